```python
import jax, jax.numpy as jnp
from jax import lax
import numpy as np

D_MODEL = 1024
BATCH = 2
SEQ = 8192
DEPTH = 1

N_META = 16
BLOCK = 128
ATTN_WIDTH = D_MODEL // 2
ATTN_HEADS = 8
ATTN_HEAD_DIM = ATTN_WIDTH // ATTN_HEADS
LRU_WIDTH = D_MODEL // 2
LRU_HEADS = 8
LRU_HEAD_DIM = LRU_WIDTH // LRU_HEADS
CONV_WIDTH = 4
LRU_C = 8.0
MIX_WIDTH = ATTN_WIDTH + LRU_WIDTH
IN_COLS = 3 * ATTN_WIDTH + 2 * LRU_WIDTH
PEER_HEADS = 8
PEER_KEY_DIM = 256
PEER_HALF = PEER_KEY_DIM // 2
N_KEYS = 128
N_EXPERTS = N_KEYS * N_KEYS
PEER_TOPK = 16
DEEPNORM_ALPHA = (2.0 * DEPTH) ** 0.25
DEEPNORM_BETA = (8.0 * DEPTH) ** -0.25
NORM_EPS = 1e-5

kernel_name = "hymba_sb_rglru_peer_deepnorm"


def layer_norm(x, g, b):
    xf = x.astype(jnp.float32)
    mu = jnp.mean(xf, axis=-1, keepdims=True)
    var = jnp.mean(jnp.square(xf - mu), axis=-1, keepdims=True)
    return ((xf - mu) * lax.rsqrt(var + NORM_EPS) * g.astype(jnp.float32) + b.astype(jnp.float32)).astype(x.dtype)


def rms_norm(x, g):
    xf = x.astype(jnp.float32)
    ms = jnp.mean(jnp.square(xf), axis=-1, keepdims=True)
    return (xf * lax.rsqrt(ms + NORM_EPS) * g.astype(jnp.float32)).astype(x.dtype)


def over_blocks(fn, *arrs):
    head = fn(*(a[:, :N_META] for a in arrs))
    n_blk = (arrs[0].shape[1] - N_META) // BLOCK

    def to_blocks(a):
        r = a[:, N_META:].reshape(a.shape[0], n_blk, BLOCK, *a.shape[2:])
        return jnp.moveaxis(r, 1, 0)

    body = lax.map(lambda blk: fn(*blk), tuple(to_blocks(a) for a in arrs))
    body = jnp.moveaxis(body, 0, 1)
    body = body.reshape(body.shape[0], n_blk * BLOCK, *body.shape[3:])
    return jnp.concatenate([head, body], axis=1)


def stick_breaking_block(q_blk, q_pos, k, v):
    z = jnp.einsum('bchd,bshd->bhcs', q_blk, k).astype(jnp.float32) * (ATTN_HEAD_DIM ** -0.5)
    k_pos = jnp.arange(k.shape[1], dtype=jnp.int32)
    mask = k_pos[None, :] < q_pos[:, None]
    log_beta = jax.nn.log_sigmoid(z)
    log_1m_beta = jnp.where(mask, jax.nn.log_sigmoid(-z), 0.0)
    suffix = lax.cumsum(log_1m_beta, axis=3, reverse=True) - log_1m_beta
    att = jnp.where(mask, jnp.exp(log_beta + suffix), 0.0)
    out = jnp.einsum('bhcs,bshd->bchd', att, v.astype(jnp.float32))
    return out.astype(q_blk.dtype)


def causal_depthwise_conv(x, w, b):
    y = lax.conv_general_dilated(
        x, w[:, None, :].astype(x.dtype), window_strides=(1,), padding=[(CONV_WIDTH - 1, 0)],
        dimension_numbers=('NWC', 'WIO', 'NWC'), feature_group_count=x.shape[-1])
    return y + b.astype(x.dtype)


def rg_lru(x, gate_a_w, gate_a_b, gate_x_w, gate_x_b, lam):
    bsz, t_len, ch = x.shape
    xh = x.reshape(bsz, t_len, LRU_HEADS, LRU_HEAD_DIM)
    r = jax.nn.sigmoid(jnp.einsum('bthi,hij->bthj', xh, gate_a_w).reshape(bsz, t_len, ch) + gate_a_b)
    i = jax.nn.sigmoid(jnp.einsum('bthi,hij->bthj', xh, gate_x_w).reshape(bsz, t_len, ch) + gate_x_b)
    log_a = -LRU_C * r.astype(jnp.float32) * jax.nn.softplus(-lam.astype(jnp.float32))
    a = jnp.exp(log_a)
    b_in = jnp.sqrt(-jnp.expm1(2.0 * log_a)) * (i * x).astype(jnp.float32)

    def combine(left, right):
        a1, b1 = left
        a2, b2 = right
        return a1 * a2, a2 * b1 + b2

    _, h = lax.associative_scan(combine, (a, b_in), axis=1)
    return h.astype(x.dtype)


def peer_block(xb, w_query, sub_keys, down, up):
    q = jnp.einsum('bcd,dhk->bchk', xb, w_query).astype(jnp.float32)
    q1, q2 = q[..., :PEER_HALF], q[..., PEER_HALF:]
    s1 = jnp.einsum('bchk,hnk->bchn', q1, sub_keys[:, 0].astype(jnp.float32))
    s2 = jnp.einsum('bchk,hnk->bchn', q2, sub_keys[:, 1].astype(jnp.float32))
    v1, i1 = lax.top_k(s1, PEER_TOPK)
    v2, i2 = lax.top_k(s2, PEER_TOPK)
    lead = v1.shape[:-1]
    cand = (v1[..., :, None] + v2[..., None, :]).reshape(*lead, PEER_TOPK * PEER_TOPK)
    cand_idx = (i1[..., :, None] * N_KEYS + i2[..., None, :]).reshape(*lead, PEER_TOPK * PEER_TOPK)
    sc, sel = lax.top_k(cand, PEER_TOPK)
    idx = jnp.take_along_axis(cand_idx, sel, axis=-1)
    g = jax.nn.softmax(sc, axis=-1)
    u_sel = down[idx]
    act = jax.nn.gelu(jnp.einsum('bchkd,bcd->bchk', u_sel, xb).astype(jnp.float32))
    w = (g * act).astype(xb.dtype)
    return jnp.einsum('bchk,bchkd->bcd', w, up[idx])


def setup_inputs(seed: int = 0) -> dict:
    key = jax.random.key(seed)
    ks = jax.random.split(key, 24)
    f32 = jnp.float32
    nrm = lambda k, shape, s: jax.random.normal(k, shape, f32) * s
    s_u = jax.random.uniform(ks[9], (DEPTH, LRU_WIDTH), f32, minval=0.9, maxval=0.999) ** (1.0 / LRU_C)
    return {
        "x": jax.random.normal(ks[0], (BATCH, SEQ, D_MODEL), f32),
        "meta_tokens": nrm(ks[1], (N_META, D_MODEL), 1.0),
        "ln_emb_g": 1.0 + nrm(ks[2], (D_MODEL,), 0.02),
        "ln_emb_b": nrm(ks[3], (D_MODEL,), 0.02),
        "w_in": nrm(ks[4], (DEPTH, D_MODEL, IN_COLS), D_MODEL ** -0.5),
        "conv_w": nrm(ks[5], (DEPTH, CONV_WIDTH, LRU_WIDTH), CONV_WIDTH ** -0.5),
        "conv_b": nrm(ks[6], (DEPTH, LRU_WIDTH), 0.01),
        "gate_a_w": nrm(ks[7], (DEPTH, LRU_HEADS, LRU_HEAD_DIM, LRU_HEAD_DIM), LRU_HEAD_DIM ** -0.5),
        "gate_a_b": nrm(ks[8], (DEPTH, LRU_WIDTH), 0.01),
        "gate_x_w": nrm(ks[10], (DEPTH, LRU_HEADS, LRU_HEAD_DIM, LRU_HEAD_DIM), LRU_HEAD_DIM ** -0.5),
        "gate_x_b": nrm(ks[11], (DEPTH, LRU_WIDTH), 0.01),
        "lru_lambda": jnp.log(s_u) - jnp.log1p(-s_u),
        "attn_norm_g": 1.0 + nrm(ks[12], (DEPTH, ATTN_WIDTH), 0.02),
        "lru_norm_g": 1.0 + nrm(ks[13], (DEPTH, LRU_WIDTH), 0.02),
        "w_out": nrm(ks[14], (DEPTH, MIX_WIDTH, D_MODEL), DEEPNORM_BETA * MIX_WIDTH ** -0.5),
        "ln1_g": 1.0 + nrm(ks[15], (DEPTH, D_MODEL), 0.02),
        "ln1_b": nrm(ks[16], (DEPTH, D_MODEL), 0.02),
        "peer_query_w": nrm(ks[17], (DEPTH, D_MODEL, PEER_HEADS, PEER_KEY_DIM), D_MODEL ** -0.5),
        "peer_sub_keys": nrm(ks[18], (DEPTH, PEER_HEADS, 2, N_KEYS, PEER_HALF), PEER_HALF ** -0.5),
        "peer_down": nrm(ks[19], (DEPTH, N_EXPERTS, D_MODEL), D_MODEL ** -0.5),
        "peer_up": nrm(ks[20], (DEPTH, N_EXPERTS, D_MODEL), DEEPNORM_BETA * PEER_HEADS ** -0.5),
        "ln2_g": 1.0 + nrm(ks[21], (DEPTH, D_MODEL), 0.02),
        "ln2_b": nrm(ks[22], (DEPTH, D_MODEL), 0.02),
    }


def reference(x, meta_tokens, ln_emb_g, ln_emb_b, w_in, conv_w, conv_b, gate_a_w, gate_a_b,
              gate_x_w, gate_x_b, lru_lambda, attn_norm_g, lru_norm_g, w_out, ln1_g, ln1_b,
              peer_query_w, peer_sub_keys, peer_down, peer_up, ln2_g, ln2_b):
    bsz = x.shape[0]
    meta = jnp.broadcast_to(meta_tokens[None].astype(x.dtype), (bsz, N_META, D_MODEL))
    h = layer_norm(jnp.concatenate([meta, x], axis=1), ln_emb_g, ln_emb_b)
    t_len = h.shape[1]
    pos = jnp.arange(t_len, dtype=jnp.int32)[None, :]
    splits = [ATTN_WIDTH, 2 * ATTN_WIDTH, 3 * ATTN_WIDTH, 3 * ATTN_WIDTH + LRU_WIDTH]
    for l in range(DEPTH):
        proj = h @ w_in[l].astype(h.dtype)
        q, k, v, xr, gr = jnp.split(proj, splits, axis=-1)
        q = q.reshape(bsz, t_len, ATTN_HEADS, ATTN_HEAD_DIM)
        k = k.reshape(bsz, t_len, ATTN_HEADS, ATTN_HEAD_DIM)
        v = v.reshape(bsz, t_len, ATTN_HEADS, ATTN_HEAD_DIM)
        attn = over_blocks(lambda qb, pb: stick_breaking_block(qb, pb[0], k, v), q, pos)
        attn = attn.reshape(bsz, t_len, ATTN_WIDTH)
        xr = causal_depthwise_conv(xr, conv_w[l], conv_b[l])
        lru = rg_lru(xr, gate_a_w[l], gate_a_b[l], gate_x_w[l], gate_x_b[l], lru_lambda[l]) * jax.nn.gelu(gr)
        mixed = jnp.concatenate([rms_norm(attn, attn_norm_g[l]), rms_norm(lru, lru_norm_g[l])], axis=-1)
        mixed = mixed @ w_out[l].astype(h.dtype)
        h = layer_norm(DEEPNORM_ALPHA * h + mixed, ln1_g[l], ln1_b[l])
        ffn = over_blocks(lambda xb: peer_block(xb, peer_query_w[l], peer_sub_keys[l], peer_down[l], peer_up[l]), h)
        h = layer_norm(DEEPNORM_ALPHA * h + ffn, ln2_g[l], ln2_b[l])
    return h[:, N_META:]
```

```python
import functools

import jax
import jax.numpy as jnp
from jax import lax
from jax.experimental import pallas as pl
from jax.experimental.pallas import tpu as pltpu

F32 = jnp.float32
BF16 = jnp.bfloat16

BLK = 128
LANES = 128
SUBLANES = 8
ATTN_HEADS = 8
LRU_HEADS = 8
CONV_WIDTH = 4
LRU_C = 8.0
PEER_HEADS = 8
PEER_TOPK = 16
NORM_EPS = 1e-5
VMEM_LIMIT = 56 * 1024 * 1024


def _layer_norm(x, g, b):
    mu = jnp.mean(x, axis=-1, keepdims=True)
    xc = x - mu
    var = jnp.mean(xc * xc, axis=-1, keepdims=True)
    return xc * lax.rsqrt(var + NORM_EPS) * g + b


def _rms_norm(x, g):
    ms = jnp.mean(x * x, axis=-1, keepdims=True)
    return x * lax.rsqrt(ms + NORM_EPS) * g


def _softplus(z):
    return jnp.maximum(z, 0.0) + jnp.log1p(jnp.exp(-jnp.abs(z)))


def _expm1(x):
    u = jnp.exp(x)
    um1 = u - 1.0
    near = jnp.where(u == 1.0, x, um1 * x / jnp.log(u))
    return jnp.where(x > -0.5, near, um1)


def _gelu_tanh(x):
    c = 0.7978845608028654
    return 0.5 * x * (1.0 + jnp.tanh(c * (x + 0.044715 * (x * x * x))))


def _inproj_body(x_ref, g_ref, b_ref, w_ref, q_ref, k_ref, v_ref, xr_ref, gr_ref, *, n_pad, tm, aw, lw, q_scale):
    i = pl.program_id(1)
    h = _layer_norm(x_ref[0], g_ref[...], b_ref[...])
    row = i * tm + lax.broadcasted_iota(jnp.int32, h.shape, 0)
    hb = jnp.where(row >= n_pad, h, 0.0).astype(BF16)

    def proj(lo, width):
        return jnp.dot(hb, w_ref[:, lo:lo + width], preferred_element_type=F32)

    q_ref[0] = (proj(0, aw) * q_scale).astype(BF16)
    k_ref[0] = proj(aw, aw).astype(BF16)
    v_ref[0] = proj(2 * aw, aw).astype(BF16)
    xr_ref[0] = proj(3 * aw, lw)
    gr_ref[0] = proj(3 * aw + lw, lw)


def _inproj(xp, g, b, w_bf, *, n_pad, tm, aw, lw, q_scale):
    bsz, tp, d = xp.shape
    cols = w_bf.shape[1]
    body = functools.partial(_inproj_body, n_pad=n_pad, tm=tm, aw=aw, lw=lw, q_scale=q_scale)
    row_spec = lambda width: pl.BlockSpec((1, tm, width), lambda bi, i: (bi, i, 0))
    return pl.pallas_call(
        body,
        grid=(bsz, tp // tm),
        in_specs=[
            row_spec(d),
            pl.BlockSpec((1, d), lambda bi, i: (0, 0)),
            pl.BlockSpec((1, d), lambda bi, i: (0, 0)),
            pl.BlockSpec((d, cols), lambda bi, i: (0, 0)),
        ],
        out_specs=[row_spec(aw), row_spec(aw), row_spec(aw), row_spec(lw), row_spec(lw)],
        out_shape=[
            jax.ShapeDtypeStruct((bsz, tp, aw), BF16),
            jax.ShapeDtypeStruct((bsz, tp, aw), BF16),
            jax.ShapeDtypeStruct((bsz, tp, aw), BF16),
            jax.ShapeDtypeStruct((bsz, tp, lw), F32),
            jax.ShapeDtypeStruct((bsz, tp, lw), F32),
        ],
        compiler_params=pltpu.CompilerParams(
            dimension_semantics=("arbitrary", "arbitrary"), vmem_limit_bytes=VMEM_LIMIT),
        name="inproj",
    )(xp, g, b, w_bf)


def _attn_body(q_ref, k_ref, v_ref, tri_ref, o_ref, *, n_pad, dh):
    i = pl.program_id(2)
    q = q_ref[0]
    lane = lax.broadcasted_iota(jnp.int32, q.shape, 1)
    row = lax.broadcasted_iota(jnp.int32, (BLK, BLK), 0)
    col = lax.broadcasted_iota(jnp.int32, (BLK, BLK), 1)
    tri = tri_ref[...]

    def tile(qh, j, acc, run, mode):
        start = pl.multiple_of(j * BLK, BLK)
        kj = k_ref[0, pl.ds(start, BLK), :]
        vj = v_ref[0, pl.ds(start, BLK), :]
        z = lax.dot_general(qh, kj, (((1,), (1,)), ((), ())), preferred_element_type=F32)
        sp = _softplus(z)
        log_beta = z - sp
        if mode == "diag":
            sp = jnp.where(col < row, sp, 0.0)
        hi = sp.astype(BF16)
        lo = (sp - hi.astype(F32)).astype(BF16)
        sums = jnp.dot(jnp.concatenate([hi, lo], axis=1), tri, preferred_element_type=F32)
        att = jnp.exp(log_beta - sums[:, :BLK] - run)
        if mode == "diag":
            att = jnp.where(col < row, att, 0.0)
        elif mode == "first":
            att = jnp.where(col >= n_pad, att, 0.0)
        acc = acc + jnp.dot(att.astype(BF16), vj, preferred_element_type=F32)
        return acc, run + sums[:, BLK:]

    accs = []
    for hh in range(2):
        qh = jnp.where((lane >= hh * dh) & (lane < (hh + 1) * dh), q, jnp.zeros_like(q))
        acc = jnp.zeros((BLK, 2 * dh), F32)
        run = jnp.zeros((BLK, BLK), F32)
        acc, run = tile(qh, i + 1, acc, run, "diag")
        acc, run = lax.fori_loop(
            0, i, lambda t, c, qh=qh: tile(qh, i - t, c[0], c[1], "inner"), (acc, run))
        acc, run = tile(qh, 0, acc, run, "first")
        accs.append(acc)
    lane_o = lax.broadcasted_iota(jnp.int32, accs[0].shape, 1)
    o_ref[0] = jnp.where(lane_o < dh, accs[0], accs[1])


def _attn(q, k, v, *, n_pad, dh):
    bsz, tp, aw = q.shape
    nq = tp // BLK - 1
    pair = 2 * dh
    ridx = jnp.arange(2 * BLK)[:, None] % BLK
    cidx = jnp.arange(2 * BLK)[None, :]
    tri = ((cidx >= BLK) | (ridx > cidx)).astype(BF16)
    body = functools.partial(_attn_body, n_pad=n_pad, dh=dh)
    return pl.pallas_call(
        body,
        grid=(bsz, aw // pair, nq),
        in_specs=[
            pl.BlockSpec((1, BLK, pair), lambda b, hp, i: (b, i + 1, hp)),
            pl.BlockSpec((1, tp, pair), lambda b, hp, i: (b, 0, hp)),
            pl.BlockSpec((1, tp, pair), lambda b, hp, i: (b, 0, hp)),
            pl.BlockSpec((2 * BLK, 2 * BLK), lambda b, hp, i: (0, 0)),
        ],
        out_specs=pl.BlockSpec((1, BLK, pair), lambda b, hp, i: (b, i, hp)),
        out_shape=jax.ShapeDtypeStruct((bsz, nq * BLK, aw), F32),
        compiler_params=pltpu.CompilerParams(
            dimension_semantics=("arbitrary", "arbitrary", "arbitrary"), vmem_limit_bytes=VMEM_LIMIT),
        name="attn",
    )(q, k, v, tri)


def _lru_body(xr_ref, gr_ref, cw_ref, cb_ref, wg_ref, bg_ref, lam_ref, o_ref, xbuf, hprev, *, n_pad, lw):
    c = pl.program_id(1)

    @pl.when(c == 0)
    def _():
        xbuf[0:SUBLANES, :] = jnp.zeros((SUBLANES, lw), F32)
        hprev[...] = jnp.zeros_like(hprev)

    x = xr_ref[0]
    xbuf[SUBLANES:SUBLANES + BLK, :] = x
    xc = cw_ref[CONV_WIDTH - 1:CONV_WIDTH, :] * x + cb_ref[...]
    for tap in range(1, CONV_WIDTH):
        xs = xbuf[SUBLANES - tap:SUBLANES - tap + BLK, :]
        xc = xc + cw_ref[CONV_WIDTH - 1 - tap:CONV_WIDTH - tap, :] * xs
    xbuf[0:SUBLANES, :] = x[BLK - SUBLANES:, :]

    gates = jnp.dot(xc.astype(BF16), wg_ref[...], preferred_element_type=F32) + bg_ref[...]
    r = jax.nn.sigmoid(gates[:, :lw])
    ig = jax.nn.sigmoid(gates[:, lw:])
    log_a = (-LRU_C) * r * _softplus(-lam_ref[...])
    a = jnp.exp(log_a)
    b = jnp.sqrt(-_expm1(2.0 * log_a)) * (ig * xc)
    row = lax.broadcasted_iota(jnp.int32, (BLK, lw), 0)
    b = jnp.where(c * BLK + row >= n_pad, b, 0.0)

    d = 1
    while d < BLK:
        a_s = pltpu.roll(a, d, axis=0)
        b_s = pltpu.roll(b, d, axis=0)
        m = row >= d
        b = jnp.where(m, a * b_s + b, b)
        a = jnp.where(m, a * a_s, a)
        d *= 2
    h = a * hprev[...] + b
    hprev[...] = h[BLK - 1:BLK, :]
    o_ref[0] = h * _gelu_tanh(gr_ref[0])


def _lru(xr, gr, cw, cb, wg_bf, bg, lam, *, n_pad):
    bsz, tp, lw = xr.shape
    nblk = tp // BLK
    body = functools.partial(_lru_body, n_pad=n_pad, lw=lw)
    vec = lambda rows, width: pl.BlockSpec((rows, width), lambda b, c: (0, 0))
    return pl.pallas_call(
        body,
        grid=(bsz, nblk),
        in_specs=[
            pl.BlockSpec((1, BLK, lw), lambda b, c: (b, c, 0)),
            pl.BlockSpec((1, BLK, lw), lambda b, c: (b, c, 0)),
            vec(CONV_WIDTH, lw), vec(1, lw), vec(lw, 2 * lw), vec(1, 2 * lw), vec(1, lw),
        ],
        out_specs=pl.BlockSpec((1, BLK, lw), lambda b, c: (b, jnp.maximum(c - 1, 0), 0)),
        out_shape=jax.ShapeDtypeStruct((bsz, tp - BLK, lw), F32),
        scratch_shapes=[pltpu.VMEM((SUBLANES + BLK, lw), F32), pltpu.VMEM((1, lw), F32)],
        compiler_params=pltpu.CompilerParams(
            dimension_semantics=("arbitrary", "arbitrary"), vmem_limit_bytes=VMEM_LIMIT),
        name="lru",
    )(xr, gr, cw, cb, wg_bf, bg, lam)


def _outproj_body(attn_ref, lru_ref, x_ref, ga_ref, gl_ref, wo_ref, ge_ref, be_ref, g1_ref, b1_ref, o_ref,
                  *, alpha, aw):
    na = _rms_norm(attn_ref[0], ga_ref[...]).astype(BF16)
    nl = _rms_norm(lru_ref[0], gl_ref[...]).astype(BF16)
    mixed = jnp.dot(na, wo_ref[:aw, :], preferred_element_type=F32)
    mixed = mixed + jnp.dot(nl, wo_ref[aw:, :], preferred_element_type=F32)
    h = _layer_norm(x_ref[0], ge_ref[...], be_ref[...])
    h1 = _layer_norm(alpha * h + mixed, g1_ref[...], b1_ref[...])
    o_ref[...] = h1.T


def _outproj(attn, lru, x, ga, gl, wo_bf, ge, be, g1, b1, *, alpha, tm):
    bsz, seq, d = x.shape
    aw = attn.shape[-1]
    lw = lru.shape[-1]
    nt = seq // tm
    body = functools.partial(_outproj_body, alpha=alpha, aw=aw)
    vec = lambda width: pl.BlockSpec((1, width), lambda b, i: (0, 0))
    return pl.pallas_call(
        body,
        grid=(bsz, nt),
        in_specs=[
            pl.BlockSpec((1, tm, aw), lambda b, i: (b, i, 0)),
            pl.BlockSpec((1, tm, lw), lambda b, i: (b, i, 0)),
            pl.BlockSpec((1, tm, d), lambda b, i: (b, i, 0)),
            vec(aw), vec(lw),
            pl.BlockSpec((aw + lw, d), lambda b, i: (0, 0)),
            vec(d), vec(d), vec(d), vec(d),
        ],
        out_specs=pl.BlockSpec((d, tm), lambda b, i: (0, b * nt + i)),
        out_shape=jax.ShapeDtypeStruct((d, bsz * seq), F32),
        compiler_params=pltpu.CompilerParams(
            dimension_semantics=("arbitrary", "arbitrary"), vmem_limit_bytes=VMEM_LIMIT),
        name="outproj",
    )(attn, lru, x, ga, gl, wo_bf, ge, be, g1, b1)


def _top_distinct(s, k):
    rows = lax.broadcasted_iota(jnp.int32, (k, s.shape[1]), 0)

    def step(r, carry):
        s, vals = carry
        m = jnp.max(s, axis=0, keepdims=True)
        vals = jnp.where(rows == r, m, vals)
        return jnp.where(s == m, -jnp.inf, s), vals

    _, vals = lax.fori_loop(0, k, step, (s, jnp.zeros((k, s.shape[1]), F32)))
    return vals


def _peer_body(xt_ref, wq_ref, sk_ref, d_ref, ut_ref, g2_ref, b2_ref, o_ref,
               xb_scr, s1_scr, s2_scr, e1_scr, e2_scr, thr_scr, p_scr, a_scr, acc_scr,
               *, alpha, nk, half, tm, cpg):
    j = pl.program_id(1)
    nlt = tm // LANES

    @pl.when(j == 0)
    def _route():
        xb = xt_ref[...].astype(BF16)
        xb_scr[...] = xb
        acc_scr[...] = jnp.zeros_like(acc_scr)

        def head(h, carry):
            qo = pl.multiple_of(h * 2 * half, 2 * half)
            qh = jnp.dot(wq_ref[pl.ds(qo, 2 * half), :], xb, preferred_element_type=F32)
            s1 = jnp.dot(sk_ref[h, 0], qh[:half].astype(BF16), preferred_element_type=F32)
            s2 = jnp.dot(sk_ref[h, 1], qh[half:].astype(BF16), preferred_element_type=F32)
            v1 = _top_distinct(s1, PEER_TOPK)
            v2 = _top_distinct(s2, PEER_TOPK)
            cand = jnp.concatenate([v1[r:r + 1] + v2 for r in range(PEER_TOPK)], axis=0)
            thr = _top_distinct(cand, PEER_TOPK)[PEER_TOPK - 1:PEER_TOPK]
            top = v1[0:1] + v2[0:1]
            zsum = jnp.sum(jnp.where(cand >= thr, jnp.exp(cand - top), 0.0), axis=0, keepdims=True)
            s1_scr[h] = s1
            s2_scr[h] = s2
            e1_scr[h] = jnp.exp(s1 - v1[0:1])
            e2_scr[h] = jnp.exp(s2 - v2[0:1]) / zsum
            thr_scr[h] = jnp.broadcast_to(thr, (SUBLANES, tm))
            return carry

        lax.fori_loop(0, PEER_HEADS, head, 0)

    p_scr[...] = jnp.dot(d_ref[...], xb_scr[...], preferred_element_type=F32)

    c0 = pl.multiple_of(j * cpg, SUBLANES)
    for lt in range(nlt):
        ls = slice(lt * LANES, (lt + 1) * LANES)
        s1rows = [s1_scr[h, pl.ds(c0, cpg), ls] for h in range(PEER_HEADS)]
        e1rows = [e1_scr[h, pl.ds(c0, cpg), ls] for h in range(PEER_HEADS)]
        for cc in range(cpg):
            g = jnp.zeros((nk, LANES), F32)
            for h in range(PEER_HEADS):
                sel = (s2_scr[h, :, ls] + s1rows[h][cc:cc + 1]) >= thr_scr[h, 0:1, ls]
                g = g + jnp.where(sel, e2_scr[h, :, ls], 0.0) * e1rows[h][cc:cc + 1]
            ps = slice(cc * nk, (cc + 1) * nk)
            a_scr[ps, ls] = (g * _gelu_tanh(p_scr[ps, ls])).astype(BF16)
    acc_scr[...] += jnp.dot(ut_ref[...], a_scr[...], preferred_element_type=F32)

    @pl.when(j == pl.num_programs(1) - 1)
    def _finish():
        y = (alpha * xt_ref[...] + acc_scr[...]).T
        o_ref[...] = _layer_norm(y, g2_ref[...], b2_ref[...])


def _peer(h1t, wq_bf, sk_bf, down_bf, upt_bf, g2, b2, *, alpha, tm, cpg):
    d, n = h1t.shape
    nk, half = sk_bf.shape[2], sk_bf.shape[3]
    n_exp = down_bf.shape[0]
    assert cpg == SUBLANES and nk % cpg == 0
    eg = cpg * nk
    body = functools.partial(_peer_body, alpha=alpha, nk=nk, half=half, tm=tm, cpg=cpg)
    head_scr = lambda: pltpu.VMEM((PEER_HEADS, nk, tm), F32)
    return pl.pallas_call(
        body,
        grid=(n // tm, n_exp // eg),
        in_specs=[
            pl.BlockSpec((d, tm), lambda i, j: (0, i)),
            pl.BlockSpec(wq_bf.shape, lambda i, j: (0, 0)),
            pl.BlockSpec(sk_bf.shape, lambda i, j: (0, 0, 0, 0)),
            pl.BlockSpec((eg, d), lambda i, j: (j, 0)),
            pl.BlockSpec((d, eg), lambda i, j: (0, j)),
            pl.BlockSpec((1, d), lambda i, j: (0, 0)),
            pl.BlockSpec((1, d), lambda i, j: (0, 0)),
        ],
        out_specs=pl.BlockSpec((tm, d), lambda i, j: (i, 0)),
        out_shape=jax.ShapeDtypeStruct((n, d), F32),
        scratch_shapes=[
            pltpu.VMEM((d, tm), BF16),
            head_scr(), head_scr(), head_scr(), head_scr(),
            pltpu.VMEM((PEER_HEADS, SUBLANES, tm), F32),
            pltpu.VMEM((eg, tm), F32),
            pltpu.VMEM((eg, tm), BF16),
            pltpu.VMEM((d, tm), F32),
        ],
        compiler_params=pltpu.CompilerParams(
            dimension_semantics=("arbitrary", "arbitrary"), vmem_limit_bytes=VMEM_LIMIT),
        name="peer",
    )(h1t, wq_bf, sk_bf, down_bf, upt_bf, g2, b2)


def _pick_tile(n, candidates):
    for t in candidates:
        if n % t == 0:
            return t
    raise ValueError(f"no tile for {n}")


def _block_diag(w):
    nh, di, dj = w.shape
    eye = jnp.eye(nh, dtype=w.dtype)
    return (eye[:, None, :, None] * w[:, :, None, :]).reshape(nh * di, nh * dj)


def kernel(x, meta_tokens, ln_emb_g, ln_emb_b, w_in, conv_w, conv_b, gate_a_w, gate_a_b, gate_x_w, gate_x_b,
           lru_lambda, attn_norm_g, lru_norm_g, w_out, ln1_g, ln1_b, peer_query_w, peer_sub_keys, peer_down,
           peer_up, ln2_g, ln2_b):
    bsz, seq, d = x.shape
    depth = w_in.shape[0]
    assert depth == 1, "single-layer stack only"
    n_meta = meta_tokens.shape[0]
    aw = attn_norm_g.shape[-1]
    lw = lru_norm_g.shape[-1]
    dh = aw // ATTN_HEADS
    assert 2 * dh == LANES and seq % BLK == 0 and n_meta <= BLK
    n_pad = BLK - n_meta
    alpha = (2.0 * depth) ** 0.25
    row = lambda v: v.reshape(1, -1)

    head = jnp.concatenate([jnp.zeros((n_pad, d), x.dtype), meta_tokens.astype(x.dtype)], axis=0)
    xp = jnp.concatenate([jnp.broadcast_to(head[None], (bsz, BLK, d)), x], axis=1)
    tp = seq + BLK

    q, k, v, xr, gr = _inproj(
        xp, row(ln_emb_g), row(ln_emb_b), w_in[0].astype(BF16),
        n_pad=n_pad, tm=_pick_tile(tp, (640, 512, 384, 256, 128)), aw=aw, lw=lw, q_scale=dh ** -0.5)

    attn = _attn(q, k, v, n_pad=n_pad, dh=dh)

    wg = jnp.concatenate([_block_diag(gate_a_w[0]), _block_diag(gate_x_w[0])], axis=1).astype(BF16)
    bg = jnp.concatenate([gate_a_b[0], gate_x_b[0]]).reshape(1, -1)
    lru = _lru(xr, gr, conv_w[0], row(conv_b[0]), wg, bg, row(lru_lambda[0]), n_pad=n_pad)

    h1t = _outproj(attn, lru, x, row(attn_norm_g[0]), row(lru_norm_g[0]), w_out[0].astype(BF16),
                   row(ln_emb_g), row(ln_emb_b), row(ln1_g[0]), row(ln1_b[0]),
                   alpha=alpha, tm=_pick_tile(seq, (512, 256, 128)))

    nk = peer_sub_keys.shape[3]
    wq = peer_query_w[0].reshape(d, -1).T.astype(BF16)
    out = _peer(h1t, wq, peer_sub_keys[0].astype(BF16), peer_down[0].astype(BF16),
                peer_up[0].T.astype(BF16), row(ln2_g[0]), row(ln2_b[0]),
                alpha=alpha, tm=_pick_tile(bsz * seq, (512, 256, 128)), cpg=SUBLANES)
    return out.reshape(bsz, seq, d)
```

```python
import functools

import jax
import jax.numpy as jnp
from jax import lax
from jax.experimental import pallas as pl
from jax.experimental.pallas import tpu as pltpu

F32 = jnp.float32
BF16 = jnp.bfloat16

BLK = 128
LANES = 128
SUBLANES = 8
ATTN_HEADS = 8
ATTN_QB = 4
LRU_HEADS = 8
CONV_WIDTH = 4
LRU_C = 8.0
PEER_HEADS = 8
PEER_TOPK = 16
NORM_EPS = 1e-5
VMEM_LIMIT = 56 * 1024 * 1024


def _layer_norm(x, g, b):
    mu = jnp.mean(x, axis=-1, keepdims=True)
    xc = x - mu
    var = jnp.mean(xc * xc, axis=-1, keepdims=True)
    return xc * lax.rsqrt(var + NORM_EPS) * g + b


def _rms_norm(x, g):
    ms = jnp.mean(x * x, axis=-1, keepdims=True)
    return x * lax.rsqrt(ms + NORM_EPS) * g


def _softplus(z):
    return jnp.maximum(z, 0.0) + jnp.log1p(jnp.exp(-jnp.abs(z)))


def _expm1(x):
    u = jnp.exp(x)
    um1 = u - 1.0
    near = jnp.where(u == 1.0, x, um1 * x / jnp.log(u))
    return jnp.where(x > -0.5, near, um1)


def _gelu_tanh(x):
    c = 0.7978845608028654
    return 0.5 * x * (1.0 + jnp.tanh(c * (x + 0.044715 * (x * x * x))))


def _inproj_body(x_ref, g_ref, b_ref, w_ref, q_ref, k_ref, v_ref, xr_ref, gr_ref, *, n_pad, tm, aw, lw, q_scale):
    i = pl.program_id(1)
    h = _layer_norm(x_ref[0], g_ref[...], b_ref[...])
    row = i * tm + lax.broadcasted_iota(jnp.int32, h.shape, 0)
    hb = jnp.where(row >= n_pad, h, 0.0).astype(BF16)

    def proj(lo, width):
        return jnp.dot(hb, w_ref[:, lo:lo + width], preferred_element_type=F32)

    q_ref[0] = (proj(0, aw) * q_scale).astype(BF16)
    k_ref[0] = proj(aw, aw).astype(BF16)
    v_ref[0] = proj(2 * aw, aw).astype(BF16)
    xr_ref[0] = proj(3 * aw, lw)
    gr_ref[0] = proj(3 * aw + lw, lw)


def _inproj(xp, g, b, w_bf, *, n_pad, tm, aw, lw, q_scale):
    bsz, tp, d = xp.shape
    cols = w_bf.shape[1]
    body = functools.partial(_inproj_body, n_pad=n_pad, tm=tm, aw=aw, lw=lw, q_scale=q_scale)
    row_spec = lambda width: pl.BlockSpec((1, tm, width), lambda bi, i: (bi, i, 0))
    return pl.pallas_call(
        body,
        grid=(bsz, tp // tm),
        in_specs=[
            row_spec(d),
            pl.BlockSpec((1, d), lambda bi, i: (0, 0)),
            pl.BlockSpec((1, d), lambda bi, i: (0, 0)),
            pl.BlockSpec((d, cols), lambda bi, i: (0, 0)),
        ],
        out_specs=[row_spec(aw), row_spec(aw), row_spec(aw), row_spec(lw), row_spec(lw)],
        out_shape=[
            jax.ShapeDtypeStruct((bsz, tp, aw), BF16),
            jax.ShapeDtypeStruct((bsz, tp, aw), BF16),
            jax.ShapeDtypeStruct((bsz, tp, aw), BF16),
            jax.ShapeDtypeStruct((bsz, tp, lw), F32),
            jax.ShapeDtypeStruct((bsz, tp, lw), F32),
        ],
        compiler_params=pltpu.CompilerParams(
            dimension_semantics=("arbitrary", "arbitrary"), vmem_limit_bytes=VMEM_LIMIT),
        name="inproj",
    )(xp, g, b, w_bf)


UNDERFLOW_SUM = 105.0


def _attn_body(q_ref, k_ref, v_ref, tri_ref, o_ref, acc_scr, run_scr, *, dh, qb):
    g = pl.program_id(2)
    base = (g + 1) * qb
    tri = tri_ref[...]
    row = lax.broadcasted_iota(jnp.int32, (BLK, BLK), 0)
    col = lax.broadcasted_iota(jnp.int32, (BLK, BLK), 1)
    lane = lax.broadcasted_iota(jnp.int32, (BLK, 2 * dh), 1)
    chains = [(u, hh) for u in range(qb) for hh in range(2)]

    def q_head(u, hh):
        q = q_ref[0, u * BLK:(u + 1) * BLK, :]
        return jnp.where((lane >= hh * dh) & (lane < (hh + 1) * dh), q, jnp.zeros_like(q))

    def tile(qh, j, acc, run, diag):
        start = pl.multiple_of(j * BLK, BLK)
        kj = k_ref[0, pl.ds(start, BLK), :]
        vj = v_ref[0, pl.ds(start, BLK), :]
        z = lax.dot_general(qh, kj, (((1,), (1,)), ((), ())), preferred_element_type=F32)
        sp = jnp.maximum(z, 0.0) + jnp.log(1.0 + jnp.exp(-jnp.abs(z)))
        if diag:
            sp = jnp.where(col < row, sp, 0.0)
        hi = sp.astype(BF16)
        lo = (sp - hi.astype(F32)).astype(BF16)
        sums = jnp.dot(jnp.concatenate([hi, lo], axis=1), tri, preferred_element_type=F32)
        att = jnp.exp(z - sums[:, :BLK] - run)
        if diag:
            att = jnp.where(col < row, att, 0.0)
        acc = acc + jnp.dot(att.astype(BF16), vj, preferred_element_type=F32)
        return acc, run + sums[:, BLK:]

    run_min = None
    for c, (u, hh) in enumerate(chains):
        acc, run = tile(q_head(u, hh), base + u, jnp.zeros((BLK, 2 * dh), F32), jnp.zeros((BLK, BLK), F32), True)
        acc_scr[c] = acc
        run_scr[c] = run
        run_min = run if run_min is None else jnp.minimum(run_min, run)
    done0 = (jnp.min(run_min) >= UNDERFLOW_SUM).astype(jnp.int32)

    def cond(carry):
        o, done = carry
        return (o <= base) & (done == 0)

    def body(carry):
        o, _ = carry
        run_min = None
        for c, (u, hh) in enumerate(chains):
            acc, run = tile(q_head(u, hh), base + u - o, acc_scr[c], run_scr[c], False)
            acc_scr[c] = acc
            run_scr[c] = run
            run_min = run if run_min is None else jnp.minimum(run_min, run)
        return o + 1, (jnp.min(run_min) >= UNDERFLOW_SUM).astype(jnp.int32)

    lax.while_loop(cond, body, (jnp.int32(1), done0))
    for u in range(qb):
        o_ref[0, u * BLK:(u + 1) * BLK, :] = jnp.where(lane < dh, acc_scr[2 * u], acc_scr[2 * u + 1])


def _attn(q, k, v, *, dh, qb):
    bsz, tp, aw = q.shape
    nq = tp // BLK - qb
    pair = 2 * dh
    ridx = jnp.arange(2 * BLK)[:, None] % BLK
    cidx = jnp.arange(2 * BLK)[None, :]
    tri = ((cidx >= BLK) | (ridx >= cidx)).astype(BF16)
    body = functools.partial(_attn_body, dh=dh, qb=qb)
    return pl.pallas_call(
        body,
        grid=(bsz, aw // pair, nq // qb),
        in_specs=[
            pl.BlockSpec((1, qb * BLK, pair), lambda b, hp, g: (b, g + 1, hp)),
            pl.BlockSpec((1, tp, pair), lambda b, hp, g: (b, 0, hp)),
            pl.BlockSpec((1, tp, pair), lambda b, hp, g: (b, 0, hp)),
            pl.BlockSpec((2 * BLK, 2 * BLK), lambda b, hp, g: (0, 0)),
        ],
        out_specs=pl.BlockSpec((1, qb * BLK, pair), lambda b, hp, g: (b, g, hp)),
        out_shape=jax.ShapeDtypeStruct((bsz, nq * BLK, aw), F32),
        scratch_shapes=[pltpu.VMEM((2 * qb, BLK, pair), F32), pltpu.VMEM((2 * qb, BLK, BLK), F32)],
        compiler_params=pltpu.CompilerParams(
            dimension_semantics=("arbitrary", "arbitrary", "arbitrary"), vmem_limit_bytes=VMEM_LIMIT),
        name="attn",
    )(q, k, v, tri)


def _lru_body(xr_ref, gr_ref, cw_ref, cb_ref, wg_ref, bg_ref, lam_ref, o_ref, xbuf, hprev, *, n_pad, lw):
    c = pl.program_id(1)

    @pl.when(c == 0)
    def _():
        xbuf[0:SUBLANES, :] = jnp.zeros((SUBLANES, lw), F32)
        hprev[...] = jnp.zeros_like(hprev)

    x = xr_ref[0]
    xbuf[SUBLANES:SUBLANES + BLK, :] = x
    xc = cw_ref[CONV_WIDTH - 1:CONV_WIDTH, :] * x + cb_ref[...]
    for tap in range(1, CONV_WIDTH):
        xs = xbuf[SUBLANES - tap:SUBLANES - tap + BLK, :]
        xc = xc + cw_ref[CONV_WIDTH - 1 - tap:CONV_WIDTH - tap, :] * xs
    xbuf[0:SUBLANES, :] = x[BLK - SUBLANES:, :]

    gates = jnp.dot(xc.astype(BF16), wg_ref[...], preferred_element_type=F32) + bg_ref[...]
    r = jax.nn.sigmoid(gates[:, :lw])
    ig = jax.nn.sigmoid(gates[:, lw:])
    log_a = (-LRU_C) * r * _softplus(-lam_ref[...])
    a = jnp.exp(log_a)
    b = jnp.sqrt(-_expm1(2.0 * log_a)) * (ig * xc)
    row = lax.broadcasted_iota(jnp.int32, (BLK, lw), 0)
    b = jnp.where(c * BLK + row >= n_pad, b, 0.0)

    d = 1
    while d < BLK:
        a_s = pltpu.roll(a, d, axis=0)
        b_s = pltpu.roll(b, d, axis=0)
        m = row >= d
        b = jnp.where(m, a * b_s + b, b)
        a = jnp.where(m, a * a_s, a)
        d *= 2
    h = a * hprev[...] + b
    hprev[...] = h[BLK - 1:BLK, :]
    o_ref[0] = h * _gelu_tanh(gr_ref[0])


def _lru(xr, gr, cw, cb, wg_bf, bg, lam, *, n_pad, lead):
    bsz, tp, lw = xr.shape
    nblk = tp // BLK
    body = functools.partial(_lru_body, n_pad=n_pad, lw=lw)
    vec = lambda rows, width: pl.BlockSpec((rows, width), lambda b, c: (0, 0))
    return pl.pallas_call(
        body,
        grid=(bsz, nblk),
        in_specs=[
            pl.BlockSpec((1, BLK, lw), lambda b, c: (b, c, 0)),
            pl.BlockSpec((1, BLK, lw), lambda b, c: (b, c, 0)),
            vec(CONV_WIDTH, lw), vec(1, lw), vec(lw, 2 * lw), vec(1, 2 * lw), vec(1, lw),
        ],
        out_specs=pl.BlockSpec((1, BLK, lw), lambda b, c: (b, jnp.maximum(c - lead, 0), 0)),
        out_shape=jax.ShapeDtypeStruct((bsz, tp - lead * BLK, lw), F32),
        scratch_shapes=[pltpu.VMEM((SUBLANES + BLK, lw), F32), pltpu.VMEM((1, lw), F32)],
        compiler_params=pltpu.CompilerParams(
            dimension_semantics=("arbitrary", "arbitrary"), vmem_limit_bytes=VMEM_LIMIT),
        name="lru",
    )(xr, gr, cw, cb, wg_bf, bg, lam)


def _outproj_body(attn_ref, lru_ref, x_ref, ga_ref, gl_ref, wo_ref, ge_ref, be_ref, g1_ref, b1_ref, o_ref,
                  *, alpha, aw):
    na = _rms_norm(attn_ref[0], ga_ref[...]).astype(BF16)
    nl = _rms_norm(lru_ref[0], gl_ref[...]).astype(BF16)
    mixed = jnp.dot(na, wo_ref[:aw, :], preferred_element_type=F32)
    mixed = mixed + jnp.dot(nl, wo_ref[aw:, :], preferred_element_type=F32)
    h = _layer_norm(x_ref[0], ge_ref[...], be_ref[...])
    h1 = _layer_norm(alpha * h + mixed, g1_ref[...], b1_ref[...])
    o_ref[...] = h1.T


def _outproj(attn, lru, x, ga, gl, wo_bf, ge, be, g1, b1, *, alpha, tm):
    bsz, seq, d = x.shape
    aw = attn.shape[-1]
    lw = lru.shape[-1]
    nt = seq // tm
    body = functools.partial(_outproj_body, alpha=alpha, aw=aw)
    vec = lambda width: pl.BlockSpec((1, width), lambda b, i: (0, 0))
    return pl.pallas_call(
        body,
        grid=(bsz, nt),
        in_specs=[
            pl.BlockSpec((1, tm, aw), lambda b, i: (b, i, 0)),
            pl.BlockSpec((1, tm, lw), lambda b, i: (b, i, 0)),
            pl.BlockSpec((1, tm, d), lambda b, i: (b, i, 0)),
            vec(aw), vec(lw),
            pl.BlockSpec((aw + lw, d), lambda b, i: (0, 0)),
            vec(d), vec(d), vec(d), vec(d),
        ],
        out_specs=pl.BlockSpec((d, tm), lambda b, i: (0, b * nt + i)),
        out_shape=jax.ShapeDtypeStruct((d, bsz * seq), F32),
        compiler_params=pltpu.CompilerParams(
            dimension_semantics=("arbitrary", "arbitrary"), vmem_limit_bytes=VMEM_LIMIT),
        name="outproj",
    )(attn, lru, x, ga, gl, wo_bf, ge, be, g1, b1)


def _top_distinct(s, k):
    rows = lax.broadcasted_iota(jnp.int32, (k, s.shape[1]), 0)

    def step(r, carry):
        s, vals = carry
        m = jnp.max(s, axis=0, keepdims=True)
        vals = jnp.where(rows == r, m, vals)
        return jnp.where(s == m, -jnp.inf, s), vals

    _, vals = lax.fori_loop(0, k, step, (s, jnp.zeros((k, s.shape[1]), F32)))
    return vals


def _peer_body(xt_ref, wq_ref, sk_ref, d_ref, ut_ref, g2_ref, b2_ref, o_ref,
               xb_scr, s1_scr, s2_scr, e1_scr, e2_scr, thr_scr, p_scr, a_scr, acc_scr,
               *, alpha, nk, half, tm, cpg):
    j = pl.program_id(1)
    nlt = tm // LANES

    @pl.when(j == 0)
    def _route():
        xb = xt_ref[...].astype(BF16)
        xb_scr[...] = xb
        acc_scr[...] = jnp.zeros_like(acc_scr)

        def head(h, carry):
            qo = pl.multiple_of(h * 2 * half, 2 * half)
            qh = jnp.dot(wq_ref[pl.ds(qo, 2 * half), :], xb, preferred_element_type=F32)
            s1 = jnp.dot(sk_ref[h, 0], qh[:half].astype(BF16), preferred_element_type=F32)
            s2 = jnp.dot(sk_ref[h, 1], qh[half:].astype(BF16), preferred_element_type=F32)
            v1 = _top_distinct(s1, PEER_TOPK)
            v2 = _top_distinct(s2, PEER_TOPK)
            cand = jnp.concatenate([v1[r:r + 1] + v2 for r in range(PEER_TOPK)], axis=0)
            thr = _top_distinct(cand, PEER_TOPK)[PEER_TOPK - 1:PEER_TOPK]
            top = v1[0:1] + v2[0:1]
            zsum = jnp.sum(jnp.where(cand >= thr, jnp.exp(cand - top), 0.0), axis=0, keepdims=True)
            s1_scr[h] = s1
            s2_scr[h] = s2
            e1_scr[h] = jnp.exp(s1 - v1[0:1])
            e2_scr[h] = jnp.exp(s2 - v2[0:1]) / zsum
            thr_scr[h] = jnp.broadcast_to(thr, (SUBLANES, tm))
            return carry

        lax.fori_loop(0, PEER_HEADS, head, 0)

    p_scr[...] = jnp.dot(d_ref[...], xb_scr[...], preferred_element_type=F32)

    c0 = pl.multiple_of(j * cpg, SUBLANES)
    for lt in range(nlt):
        ls = slice(lt * LANES, (lt + 1) * LANES)
        s1rows = [s1_scr[h, pl.ds(c0, cpg), ls] for h in range(PEER_HEADS)]
        e1rows = [e1_scr[h, pl.ds(c0, cpg), ls] for h in range(PEER_HEADS)]
        for cc in range(cpg):
            g = jnp.zeros((nk, LANES), F32)
            for h in range(PEER_HEADS):
                sel = (s2_scr[h, :, ls] + s1rows[h][cc:cc + 1]) >= thr_scr[h, 0:1, ls]
                g = g + jnp.where(sel, e2_scr[h, :, ls], 0.0) * e1rows[h][cc:cc + 1]
            ps = slice(cc * nk, (cc + 1) * nk)
            a_scr[ps, ls] = (g * _gelu_tanh(p_scr[ps, ls])).astype(BF16)
    acc_scr[...] += jnp.dot(ut_ref[...], a_scr[...], preferred_element_type=F32)

    @pl.when(j == pl.num_programs(1) - 1)
    def _finish():
        y = (alpha * xt_ref[...] + acc_scr[...]).T
        o_ref[...] = _layer_norm(y, g2_ref[...], b2_ref[...])


def _peer(h1t, wq_bf, sk_bf, down_bf, upt_bf, g2, b2, *, alpha, tm, cpg):
    d, n = h1t.shape
    nk, half = sk_bf.shape[2], sk_bf.shape[3]
    n_exp = down_bf.shape[0]
    assert cpg == SUBLANES and nk % cpg == 0
    eg = cpg * nk
    body = functools.partial(_peer_body, alpha=alpha, nk=nk, half=half, tm=tm, cpg=cpg)
    head_scr = lambda: pltpu.VMEM((PEER_HEADS, nk, tm), F32)
    return pl.pallas_call(
        body,
        grid=(n // tm, n_exp // eg),
        in_specs=[
            pl.BlockSpec((d, tm), lambda i, j: (0, i)),
            pl.BlockSpec(wq_bf.shape, lambda i, j: (0, 0)),
            pl.BlockSpec(sk_bf.shape, lambda i, j: (0, 0, 0, 0)),
            pl.BlockSpec((eg, d), lambda i, j: (j, 0)),
            pl.BlockSpec((d, eg), lambda i, j: (0, j)),
            pl.BlockSpec((1, d), lambda i, j: (0, 0)),
            pl.BlockSpec((1, d), lambda i, j: (0, 0)),
        ],
        out_specs=pl.BlockSpec((tm, d), lambda i, j: (i, 0)),
        out_shape=jax.ShapeDtypeStruct((n, d), F32),
        scratch_shapes=[
            pltpu.VMEM((d, tm), BF16),
            head_scr(), head_scr(), head_scr(), head_scr(),
            pltpu.VMEM((PEER_HEADS, SUBLANES, tm), F32),
            pltpu.VMEM((eg, tm), F32),
            pltpu.VMEM((eg, tm), BF16),
            pltpu.VMEM((d, tm), F32),
        ],
        compiler_params=pltpu.CompilerParams(
            dimension_semantics=("arbitrary", "arbitrary"), vmem_limit_bytes=VMEM_LIMIT),
        name="peer",
    )(h1t, wq_bf, sk_bf, down_bf, upt_bf, g2, b2)


def _pick_tile(n, candidates):
    for t in candidates:
        if n % t == 0:
            return t
    raise ValueError(f"no tile for {n}")


def _block_diag(w):
    nh, di, dj = w.shape
    eye = jnp.eye(nh, dtype=w.dtype)
    return (eye[:, None, :, None] * w[:, :, None, :]).reshape(nh * di, nh * dj)


def kernel(x, meta_tokens, ln_emb_g, ln_emb_b, w_in, conv_w, conv_b, gate_a_w, gate_a_b, gate_x_w, gate_x_b,
           lru_lambda, attn_norm_g, lru_norm_g, w_out, ln1_g, ln1_b, peer_query_w, peer_sub_keys, peer_down,
           peer_up, ln2_g, ln2_b):
    bsz, seq, d = x.shape
    depth = w_in.shape[0]
    assert depth == 1, "single-layer stack only"
    n_meta = meta_tokens.shape[0]
    aw = attn_norm_g.shape[-1]
    lw = lru_norm_g.shape[-1]
    dh = aw // ATTN_HEADS
    lead = ATTN_QB
    assert 2 * dh == LANES and seq % (ATTN_QB * BLK) == 0 and n_meta <= BLK
    n_pad = lead * BLK - n_meta
    alpha = (2.0 * depth) ** 0.25
    row = lambda v: v.reshape(1, -1)

    head = jnp.concatenate([jnp.zeros((n_pad, d), x.dtype), meta_tokens.astype(x.dtype)], axis=0)
    xp = jnp.concatenate([jnp.broadcast_to(head[None], (bsz, lead * BLK, d)), x], axis=1)
    tp = seq + lead * BLK

    q, k, v, xr, gr = _inproj(
        xp, row(ln_emb_g), row(ln_emb_b), w_in[0].astype(BF16),
        n_pad=n_pad, tm=_pick_tile(tp, (512, 256, 128)), aw=aw, lw=lw, q_scale=dh ** -0.5)

    attn = _attn(q, k, v, dh=dh, qb=ATTN_QB)

    wg = jnp.concatenate([_block_diag(gate_a_w[0]), _block_diag(gate_x_w[0])], axis=1).astype(BF16)
    bg = jnp.concatenate([gate_a_b[0], gate_x_b[0]]).reshape(1, -1)
    lru = _lru(xr, gr, conv_w[0], row(conv_b[0]), wg, bg, row(lru_lambda[0]), n_pad=n_pad, lead=lead)

    h1t = _outproj(attn, lru, x, row(attn_norm_g[0]), row(lru_norm_g[0]), w_out[0].astype(BF16),
                   row(ln_emb_g), row(ln_emb_b), row(ln1_g[0]), row(ln1_b[0]),
                   alpha=alpha, tm=_pick_tile(seq, (512, 256, 128)))

    nk = peer_sub_keys.shape[3]
    wq = peer_query_w[0].reshape(d, -1).T.astype(BF16)
    out = _peer(h1t, wq, peer_sub_keys[0].astype(BF16), peer_down[0].astype(BF16),
                peer_up[0].T.astype(BF16), row(ln2_g[0]), row(ln2_b[0]),
                alpha=alpha, tm=_pick_tile(bsz * seq, (512, 256, 128)), cpg=SUBLANES)
    return out.reshape(bsz, seq, d)
```

```python
import functools

import jax
import jax.numpy as jnp
from jax import lax
from jax.experimental import pallas as pl
from jax.experimental.pallas import tpu as pltpu

F32 = jnp.float32
BF16 = jnp.bfloat16

BLK = 128
LANES = 128
SUBLANES = 8
ATTN_HEADS = 8
ATTN_QB = 8
LRU_HEADS = 8
CONV_WIDTH = 4
LRU_C = 8.0
PEER_HEADS = 8
PEER_TOPK = 16
NORM_EPS = 1e-5
VMEM_LIMIT = 56 * 1024 * 1024


def _layer_norm(x, g, b):
    mu = jnp.mean(x, axis=-1, keepdims=True)
    xc = x - mu
    var = jnp.mean(xc * xc, axis=-1, keepdims=True)
    return xc * lax.rsqrt(var + NORM_EPS) * g + b


def _rms_norm(x, g):
    ms = jnp.mean(x * x, axis=-1, keepdims=True)
    return x * lax.rsqrt(ms + NORM_EPS) * g


def _softplus(z):
    return jnp.maximum(z, 0.0) + jnp.log1p(jnp.exp(-jnp.abs(z)))


def _expm1(x):
    u = jnp.exp(x)
    um1 = u - 1.0
    near = jnp.where(u == 1.0, x, um1 * x / jnp.log(u))
    return jnp.where(x > -0.5, near, um1)


def _gelu_tanh(x):
    c = 0.7978845608028654
    return 0.5 * x * (1.0 + jnp.tanh(c * (x + 0.044715 * (x * x * x))))


def _inproj_body(x_ref, g_ref, b_ref, w_ref, q_ref, k_ref, v_ref, xr_ref, gr_ref, *, n_pad, tm, aw, lw, q_scale):
    i = pl.program_id(1)
    h = _layer_norm(x_ref[0], g_ref[...], b_ref[...])
    row = i * tm + lax.broadcasted_iota(jnp.int32, h.shape, 0)
    hb = jnp.where(row >= n_pad, h, 0.0).astype(BF16)

    def proj(lo, width):
        return jnp.dot(hb, w_ref[:, lo:lo + width], preferred_element_type=F32)

    q_ref[0] = (proj(0, aw) * q_scale).astype(BF16)
    k_ref[0] = proj(aw, aw).astype(BF16)
    v_ref[0] = proj(2 * aw, aw).astype(BF16)
    xr_ref[0] = proj(3 * aw, lw)
    gr_ref[0] = proj(3 * aw + lw, lw)


def _inproj(xp, g, b, w_bf, *, n_pad, tm, aw, lw, q_scale):
    bsz, tp, d = xp.shape
    cols = w_bf.shape[1]
    body = functools.partial(_inproj_body, n_pad=n_pad, tm=tm, aw=aw, lw=lw, q_scale=q_scale)
    row_spec = lambda width: pl.BlockSpec((1, tm, width), lambda bi, i: (bi, i, 0))
    return pl.pallas_call(
        body,
        grid=(bsz, tp // tm),
        in_specs=[
            row_spec(d),
            pl.BlockSpec((1, d), lambda bi, i: (0, 0)),
            pl.BlockSpec((1, d), lambda bi, i: (0, 0)),
            pl.BlockSpec((d, cols), lambda bi, i: (0, 0)),
        ],
        out_specs=[row_spec(aw), row_spec(aw), row_spec(aw), row_spec(lw), row_spec(lw)],
        out_shape=[
            jax.ShapeDtypeStruct((bsz, tp, aw), BF16),
            jax.ShapeDtypeStruct((bsz, tp, aw), BF16),
            jax.ShapeDtypeStruct((bsz, tp, aw), BF16),
            jax.ShapeDtypeStruct((bsz, tp, lw), F32),
            jax.ShapeDtypeStruct((bsz, tp, lw), F32),
        ],
        compiler_params=pltpu.CompilerParams(
            dimension_semantics=("arbitrary", "arbitrary"), vmem_limit_bytes=VMEM_LIMIT),
        name="inproj",
    )(xp, g, b, w_bf)


UNDERFLOW_SUM = 105.0


def _attn_body(q_ref, k_ref, v_ref, tri_ref, o_ref, acc_scr, run_scr, *, dh, qb):
    g = pl.program_id(2)
    base = (g + 1) * qb
    tri = tri_ref[...]
    row = lax.broadcasted_iota(jnp.int32, (BLK, BLK), 0)
    col = lax.broadcasted_iota(jnp.int32, (BLK, BLK), 1)
    lane = lax.broadcasted_iota(jnp.int32, (BLK, 2 * dh), 1)
    chains = [(u, hh) for u in range(qb) for hh in range(2)]

    def q_head(u, hh):
        q = q_ref[0, u * BLK:(u + 1) * BLK, :]
        return jnp.where((lane >= hh * dh) & (lane < (hh + 1) * dh), q, jnp.zeros_like(q))

    def key_block(ref, j):
        return ref[0, pl.ds(pl.multiple_of(j * BLK, BLK), BLK), :]

    def sweep(o, diag):
        js = [base + u - o for u, _ in chains]
        zs = [lax.dot_general(q_head(u, hh), key_block(k_ref, j), (((1,), (1,)), ((), ())),
                              preferred_element_type=F32) for (u, hh), j in zip(chains, js)]
        sums = []
        for z in zs:
            sp = jnp.maximum(z, 0.0) + jnp.log(1.0 + jnp.exp(-jnp.abs(z)))
            if diag:
                sp = jnp.where(col < row, sp, 0.0)
            hi = sp.astype(BF16)
            lo = (sp - hi.astype(F32)).astype(BF16)
            sums.append(jnp.dot(jnp.concatenate([hi, lo], axis=1), tri, preferred_element_type=F32))
        run_min = None
        for c, (z, s, j) in enumerate(zip(zs, sums, js)):
            run = jnp.zeros((BLK, BLK), F32) if diag else run_scr[c]
            att = jnp.exp(z - s[:, :BLK] - run)
            if diag:
                att = jnp.where(col < row, att, 0.0)
            av = jnp.dot(att.astype(BF16), key_block(v_ref, j), preferred_element_type=F32)
            acc_scr[c] = av if diag else acc_scr[c] + av
            run = run + s[:, BLK:]
            run_scr[c] = run
            run_min = run if run_min is None else jnp.minimum(run_min, run)
        return (jnp.min(run_min) >= UNDERFLOW_SUM).astype(jnp.int32)

    def cond(carry):
        o, done = carry
        return (o <= base) & (done == 0)

    lax.while_loop(cond, lambda carry: (carry[0] + 1, sweep(carry[0], False)), (jnp.int32(1), sweep(0, True)))
    for u in range(qb):
        o_ref[0, u * BLK:(u + 1) * BLK, :] = jnp.where(lane < dh, acc_scr[2 * u], acc_scr[2 * u + 1])


def _attn(q, k, v, *, dh, qb):
    bsz, tp, aw = q.shape
    nq = tp // BLK - qb
    pair = 2 * dh
    ridx = jnp.arange(2 * BLK)[:, None] % BLK
    cidx = jnp.arange(2 * BLK)[None, :]
    tri = ((cidx >= BLK) | (ridx >= cidx)).astype(BF16)
    body = functools.partial(_attn_body, dh=dh, qb=qb)
    return pl.pallas_call(
        body,
        grid=(bsz, aw // pair, nq // qb),
        in_specs=[
            pl.BlockSpec((1, qb * BLK, pair), lambda b, hp, g: (b, g + 1, hp)),
            pl.BlockSpec((1, tp, pair), lambda b, hp, g: (b, 0, hp)),
            pl.BlockSpec((1, tp, pair), lambda b, hp, g: (b, 0, hp)),
            pl.BlockSpec((2 * BLK, 2 * BLK), lambda b, hp, g: (0, 0)),
        ],
        out_specs=pl.BlockSpec((1, qb * BLK, pair), lambda b, hp, g: (b, g, hp)),
        out_shape=jax.ShapeDtypeStruct((bsz, nq * BLK, aw), F32),
        scratch_shapes=[pltpu.VMEM((2 * qb, BLK, pair), F32), pltpu.VMEM((2 * qb, BLK, BLK), F32)],
        compiler_params=pltpu.CompilerParams(
            dimension_semantics=("arbitrary", "arbitrary", "arbitrary"), vmem_limit_bytes=VMEM_LIMIT),
        name="attn",
    )(q, k, v, tri)


def _lru_body(xr_ref, gr_ref, cw_ref, cb_ref, wg_ref, bg_ref, lam_ref, o_ref, xbuf, hprev, *, n_pad, lw):
    c = pl.program_id(1)

    @pl.when(c == 0)
    def _():
        xbuf[0:SUBLANES, :] = jnp.zeros((SUBLANES, lw), F32)
        hprev[...] = jnp.zeros_like(hprev)

    x = xr_ref[0]
    xbuf[SUBLANES:SUBLANES + BLK, :] = x
    xc = cw_ref[CONV_WIDTH - 1:CONV_WIDTH, :] * x + cb_ref[...]
    for tap in range(1, CONV_WIDTH):
        xs = xbuf[SUBLANES - tap:SUBLANES - tap + BLK, :]
        xc = xc + cw_ref[CONV_WIDTH - 1 - tap:CONV_WIDTH - tap, :] * xs
    xbuf[0:SUBLANES, :] = x[BLK - SUBLANES:, :]

    gates = jnp.dot(xc.astype(BF16), wg_ref[...], preferred_element_type=F32) + bg_ref[...]
    r = jax.nn.sigmoid(gates[:, :lw])
    ig = jax.nn.sigmoid(gates[:, lw:])
    log_a = (-LRU_C) * r * _softplus(-lam_ref[...])
    a = jnp.exp(log_a)
    b = jnp.sqrt(-_expm1(2.0 * log_a)) * (ig * xc)
    row = lax.broadcasted_iota(jnp.int32, (BLK, lw), 0)
    b = jnp.where(c * BLK + row >= n_pad, b, 0.0)

    d = 1
    while d < BLK:
        a_s = pltpu.roll(a, d, axis=0)
        b_s = pltpu.roll(b, d, axis=0)
        m = row >= d
        b = jnp.where(m, a * b_s + b, b)
        a = jnp.where(m, a * a_s, a)
        d *= 2
    h = a * hprev[...] + b
    hprev[...] = h[BLK - 1:BLK, :]
    o_ref[0] = h * _gelu_tanh(gr_ref[0])


def _lru(xr, gr, cw, cb, wg_bf, bg, lam, *, n_pad, lead):
    bsz, tp, lw = xr.shape
    nblk = tp // BLK
    body = functools.partial(_lru_body, n_pad=n_pad, lw=lw)
    vec = lambda rows, width: pl.BlockSpec((rows, width), lambda b, c: (0, 0))
    return pl.pallas_call(
        body,
        grid=(bsz, nblk),
        in_specs=[
            pl.BlockSpec((1, BLK, lw), lambda b, c: (b, c, 0)),
            pl.BlockSpec((1, BLK, lw), lambda b, c: (b, c, 0)),
            vec(CONV_WIDTH, lw), vec(1, lw), vec(lw, 2 * lw), vec(1, 2 * lw), vec(1, lw),
        ],
        out_specs=pl.BlockSpec((1, BLK, lw), lambda b, c: (b, jnp.maximum(c - lead, 0), 0)),
        out_shape=jax.ShapeDtypeStruct((bsz, tp - lead * BLK, lw), F32),
        scratch_shapes=[pltpu.VMEM((SUBLANES + BLK, lw), F32), pltpu.VMEM((1, lw), F32)],
        compiler_params=pltpu.CompilerParams(
            dimension_semantics=("arbitrary", "arbitrary"), vmem_limit_bytes=VMEM_LIMIT),
        name="lru",
    )(xr, gr, cw, cb, wg_bf, bg, lam)


def _outproj_body(attn_ref, lru_ref, x_ref, ga_ref, gl_ref, wo_ref, ge_ref, be_ref, g1_ref, b1_ref, o_ref,
                  *, alpha, aw):
    na = _rms_norm(attn_ref[0], ga_ref[...]).astype(BF16)
    nl = _rms_norm(lru_ref[0], gl_ref[...]).astype(BF16)
    mixed = jnp.dot(na, wo_ref[:aw, :], preferred_element_type=F32)
    mixed = mixed + jnp.dot(nl, wo_ref[aw:, :], preferred_element_type=F32)
    h = _layer_norm(x_ref[0], ge_ref[...], be_ref[...])
    h1 = _layer_norm(alpha * h + mixed, g1_ref[...], b1_ref[...])
    o_ref[...] = h1.T


def _outproj(attn, lru, x, ga, gl, wo_bf, ge, be, g1, b1, *, alpha, tm):
    bsz, seq, d = x.shape
    aw = attn.shape[-1]
    lw = lru.shape[-1]
    nt = seq // tm
    body = functools.partial(_outproj_body, alpha=alpha, aw=aw)
    vec = lambda width: pl.BlockSpec((1, width), lambda b, i: (0, 0))
    return pl.pallas_call(
        body,
        grid=(bsz, nt),
        in_specs=[
            pl.BlockSpec((1, tm, aw), lambda b, i: (b, i, 0)),
            pl.BlockSpec((1, tm, lw), lambda b, i: (b, i, 0)),
            pl.BlockSpec((1, tm, d), lambda b, i: (b, i, 0)),
            vec(aw), vec(lw),
            pl.BlockSpec((aw + lw, d), lambda b, i: (0, 0)),
            vec(d), vec(d), vec(d), vec(d),
        ],
        out_specs=pl.BlockSpec((d, tm), lambda b, i: (0, b * nt + i)),
        out_shape=jax.ShapeDtypeStruct((d, bsz * seq), F32),
        compiler_params=pltpu.CompilerParams(
            dimension_semantics=("arbitrary", "arbitrary"), vmem_limit_bytes=VMEM_LIMIT),
        name="outproj",
    )(attn, lru, x, ga, gl, wo_bf, ge, be, g1, b1)


def _top_k_ranks(s, k, exact):
    rows, n = s.shape
    row_id = lax.broadcasted_iota(jnp.int32, (rows, n), 0)
    vals = []
    rank = jnp.full((rows, n), float(k), F32)
    for r in range(k):
        m = jnp.max(s, axis=0, keepdims=True)
        hit = s == m
        if exact:
            first = jnp.min(jnp.where(hit, row_id, rows), axis=0, keepdims=True)
            hit = row_id == first
        vals.append(m)
        rank = jnp.where(hit, float(r), rank)
        s = jnp.where(hit, -jnp.inf, s)
    return jnp.concatenate(vals, axis=0), rank


def _peer_route_head(h, xb, wq_ref, sk_ref, rank2_scr, nsel_scr, e1_scr, e2_scr, *, half, exact):
    k = PEER_TOPK
    qo = pl.multiple_of(h * 2 * half, 2 * half)
    qh = jnp.dot(wq_ref[pl.ds(qo, 2 * half), :], xb, preferred_element_type=F32)
    s1_all = jnp.dot(sk_ref[h, 0], qh[:half].astype(BF16), preferred_element_type=F32)
    s2_all = jnp.dot(sk_ref[h, 1], qh[half:].astype(BF16), preferred_element_type=F32)
    widths = [k // (r1 + 1) for r1 in range(k)]
    pad = -sum(widths) % SUBLANES
    count = lambda hit: jnp.sum(jnp.where(hit, 1.0, 0.0), axis=0, keepdims=True)
    ok = None
    for lt in range(s1_all.shape[1] // LANES):
        ls = slice(lt * LANES, (lt + 1) * LANES)
        s1, s2 = s1_all[:, ls], s2_all[:, ls]
        v1, rank1 = _top_k_ranks(s1, k, exact)
        v2, rank2 = _top_k_ranks(s2, k, exact)
        cand = jnp.concatenate([v1[r1:r1 + 1] + v2[:w] for r1, w in enumerate(widths)]
                               + [jnp.full((pad, LANES), -jnp.inf, F32)], axis=0)
        _, rank_c = _top_k_ranks(cand, k, exact)
        picked = rank_c < k
        zsum = jnp.sum(jnp.where(picked, jnp.exp(cand - (v1[0:1] + v2[0:1])), 0.0), axis=0, keepdims=True)
        nsel = jnp.zeros_like(s1)
        off = 0
        for r1, w in enumerate(widths):
            nsel = jnp.where(rank1 == float(r1), count(picked[off:off + w]), nsel)
            off += w
        rank2_scr[h, :, ls] = rank2.astype(BF16)
        nsel_scr[h, :, :, ls] = nsel.reshape(nsel_scr.shape[1:3] + (LANES,))
        e1_scr[h, :, :, ls] = jnp.exp(s1 - v1[0:1]).reshape(e1_scr.shape[1:3] + (LANES,))
        e2_scr[h, :, ls] = (jnp.exp(s2 - v2[0:1]) / zsum).astype(BF16)
        ok_lt = (count(rank1 < k) == k) & (count(rank2 < k) == k) & (count(picked) == k)
        ok = ok_lt if ok is None else ok & ok_lt
    return jnp.min(jnp.where(ok, 1, 0))


def _peer_body(xt_ref, wq_ref, sk_ref, d_ref, ut_ref, g2_ref, b2_ref, o_ref,
               xb_scr, rank2_scr, nsel_scr, e1_scr, e2_scr, a_scr, acc_scr,
               *, alpha, nk, half, tm, cpg):
    j = pl.program_id(1)
    nlt = tm // LANES
    rows = 2 * SUBLANES

    @pl.when(j == 0)
    def _route():
        xb = xt_ref[...].astype(BF16)
        xb_scr[...] = xb
        acc_scr[...] = jnp.zeros_like(acc_scr)
        tables = (rank2_scr, nsel_scr, e1_scr, e2_scr)

        def route_all(exact):
            return lax.fori_loop(0, PEER_HEADS, lambda h, ok: jnp.minimum(ok, _peer_route_head(
                h, xb, wq_ref, sk_ref, *tables, half=half, exact=exact)), jnp.int32(1))

        unambiguous = route_all(False)

        @pl.when(unambiguous == 0)
        def _():
            route_all(True)

    xb = xb_scr[...]
    pieces = cpg // 2
    er = 2 * nk

    def row_bf16(tab, h, cc, ls):
        return jnp.broadcast_to(tab[h, j, cc:cc + 1, ls], (rows, LANES)).astype(BF16)

    p_next = jnp.dot(d_ref[0:er, :], xb, preferred_element_type=F32)
    for m in range(pieces):
        p_cur = p_next
        if m + 1 < pieces:
            p_next = jnp.dot(d_ref[(m + 1) * er:(m + 2) * er, :], xb, preferred_element_type=F32)
        for lt in range(nlt):
            ls = slice(lt * LANES, (lt + 1) * LANES)
            nrg = nk // rows
            g = [[jnp.zeros((rows, LANES), BF16) for _ in range(2)] for _ in range(nrg)]
            for h in range(PEER_HEADS):
                nrow = [row_bf16(nsel_scr, h, 2 * m + c2, ls) for c2 in range(2)]
                erow = [row_bf16(e1_scr, h, 2 * m + c2, ls) for c2 in range(2)]
                for rg in range(nrg):
                    r2 = rank2_scr[h, rg * rows:(rg + 1) * rows, ls]
                    e2 = e2_scr[h, rg * rows:(rg + 1) * rows, ls]
                    for c2 in range(2):
                        w = jnp.where(r2 < nrow[c2], e2, jnp.zeros_like(e2))
                        g[rg][c2] = g[rg][c2] + w * erow[c2]
            for rg in range(nrg):
                for c2 in range(2):
                    act = _gelu_tanh(p_cur[c2 * nk + rg * rows:c2 * nk + (rg + 1) * rows, ls])
                    a0 = (2 * m + c2) * nk + rg * rows
                    a_scr[a0:a0 + rows, ls] = g[rg][c2] * act.astype(BF16)
        if m % 2 == 1:
            e0 = (m - 1) * er
            acc_scr[...] += jnp.dot(ut_ref[:, e0:e0 + 2 * er], a_scr[e0:e0 + 2 * er, :], preferred_element_type=F32)

    @pl.when(j == pl.num_programs(1) - 1)
    def _finish():
        y = (alpha * xt_ref[...] + acc_scr[...]).T
        o_ref[...] = _layer_norm(y, g2_ref[...], b2_ref[...])


def _peer(h1t, wq_bf, sk_bf, down_bf, upt_bf, g2, b2, *, alpha, tm, cpg):
    d, n = h1t.shape
    nk, half = sk_bf.shape[2], sk_bf.shape[3]
    n_exp = down_bf.shape[0]
    assert cpg == SUBLANES and nk % (2 * SUBLANES) == 0
    eg = cpg * nk
    ng = n_exp // eg
    body = functools.partial(_peer_body, alpha=alpha, nk=nk, half=half, tm=tm, cpg=cpg)
    head_scr = lambda: pltpu.VMEM((PEER_HEADS, nk, tm), BF16)
    group_scr = lambda: pltpu.VMEM((PEER_HEADS, nk // cpg, cpg, tm), F32)
    return pl.pallas_call(
        body,
        grid=(n // tm, ng),
        in_specs=[
            pl.BlockSpec((d, tm), lambda i, j: (0, i), pipeline_mode=pl.Buffered(1)),
            pl.BlockSpec(wq_bf.shape, lambda i, j: (0, 0), pipeline_mode=pl.Buffered(1)),
            pl.BlockSpec(sk_bf.shape, lambda i, j: (0, 0, 0, 0), pipeline_mode=pl.Buffered(1)),
            pl.BlockSpec((eg, d), lambda i, j: (j, 0)),
            pl.BlockSpec((d, eg), lambda i, j: (0, j)),
            pl.BlockSpec((1, d), lambda i, j: (0, 0), pipeline_mode=pl.Buffered(1)),
            pl.BlockSpec((1, d), lambda i, j: (0, 0), pipeline_mode=pl.Buffered(1)),
        ],
        out_specs=pl.BlockSpec((tm, d), lambda i, j: (i, 0)),
        out_shape=jax.ShapeDtypeStruct((n, d), F32),
        scratch_shapes=[
            pltpu.VMEM((d, tm), BF16),
            head_scr(), group_scr(), group_scr(), head_scr(),
            pltpu.VMEM((eg, tm), BF16),
            pltpu.VMEM((d, tm), F32),
        ],
        compiler_params=pltpu.CompilerParams(
            dimension_semantics=("arbitrary", "arbitrary"), vmem_limit_bytes=VMEM_LIMIT),
        name="peer",
    )(h1t, wq_bf, sk_bf, down_bf, upt_bf, g2, b2)


def _pick_tile(n, candidates):
    for t in candidates:
        if n % t == 0:
            return t
    raise ValueError(f"no tile for {n}")


def _block_diag(w):
    nh, di, dj = w.shape
    eye = jnp.eye(nh, dtype=w.dtype)
    return (eye[:, None, :, None] * w[:, :, None, :]).reshape(nh * di, nh * dj)


def kernel(x, meta_tokens, ln_emb_g, ln_emb_b, w_in, conv_w, conv_b, gate_a_w, gate_a_b, gate_x_w, gate_x_b,
           lru_lambda, attn_norm_g, lru_norm_g, w_out, ln1_g, ln1_b, peer_query_w, peer_sub_keys, peer_down,
           peer_up, ln2_g, ln2_b):
    bsz, seq, d = x.shape
    depth = w_in.shape[0]
    assert depth == 1, "single-layer stack only"
    n_meta = meta_tokens.shape[0]
    aw = attn_norm_g.shape[-1]
    lw = lru_norm_g.shape[-1]
    dh = aw // ATTN_HEADS
    lead = ATTN_QB
    assert 2 * dh == LANES and seq % (ATTN_QB * BLK) == 0 and n_meta <= BLK
    n_pad = lead * BLK - n_meta
    alpha = (2.0 * depth) ** 0.25
    row = lambda v: v.reshape(1, -1)

    head = jnp.concatenate([jnp.zeros((n_pad, d), x.dtype), meta_tokens.astype(x.dtype)], axis=0)
    xp = jnp.concatenate([jnp.broadcast_to(head[None], (bsz, lead * BLK, d)), x], axis=1)
    tp = seq + lead * BLK

    q, k, v, xr, gr = _inproj(
        xp, row(ln_emb_g), row(ln_emb_b), w_in[0].astype(BF16),
        n_pad=n_pad, tm=_pick_tile(tp, (512, 256, 128)), aw=aw, lw=lw, q_scale=dh ** -0.5)

    attn = _attn(q, k, v, dh=dh, qb=ATTN_QB)

    wg = jnp.concatenate([_block_diag(gate_a_w[0]), _block_diag(gate_x_w[0])], axis=1).astype(BF16)
    bg = jnp.concatenate([gate_a_b[0], gate_x_b[0]]).reshape(1, -1)
    lru = _lru(xr, gr, conv_w[0], row(conv_b[0]), wg, bg, row(lru_lambda[0]), n_pad=n_pad, lead=lead)

    h1t = _outproj(attn, lru, x, row(attn_norm_g[0]), row(lru_norm_g[0]), w_out[0].astype(BF16),
                   row(ln_emb_g), row(ln_emb_b), row(ln1_g[0]), row(ln1_b[0]),
                   alpha=alpha, tm=_pick_tile(seq, (512, 256, 128)))

    nk = peer_sub_keys.shape[3]
    wq = peer_query_w[0].reshape(d, -1).T.astype(BF16)
    out = _peer(h1t, wq, peer_sub_keys[0].astype(BF16), peer_down[0].astype(BF16),
                peer_up[0].T.astype(BF16), row(ln2_g[0]), row(ln2_b[0]),
                alpha=alpha, tm=_pick_tile(bsz * seq, (1024, 512, 256, 128)), cpg=SUBLANES)
    return out.reshape(bsz, seq, d)
```

```python
import functools

import jax
import jax.numpy as jnp
from jax import lax
from jax.experimental import pallas as pl
from jax.experimental.pallas import tpu as pltpu

F32 = jnp.float32
BF16 = jnp.bfloat16

BLK = 128
LANES = 128
SUBLANES = 8
ATTN_HEADS = 8
ATTN_QB = 8
LRU_HEADS = 8
CONV_WIDTH = 4
LRU_C = 8.0
PEER_HEADS = 8
PEER_TOPK = 16
NORM_EPS = 1e-5
VMEM_LIMIT = 56 * 1024 * 1024


def _layer_norm(x, g, b):
    mu = jnp.mean(x, axis=-1, keepdims=True)
    xc = x - mu
    var = jnp.mean(xc * xc, axis=-1, keepdims=True)
    return xc * lax.rsqrt(var + NORM_EPS) * g + b


def _rms_norm(x, g):
    ms = jnp.mean(x * x, axis=-1, keepdims=True)
    return x * lax.rsqrt(ms + NORM_EPS) * g


def _softplus(z):
    return jnp.maximum(z, 0.0) + jnp.log1p(jnp.exp(-jnp.abs(z)))


def _expm1(x):
    u = jnp.exp(x)
    um1 = u - 1.0
    near = jnp.where(u == 1.0, x, um1 * x / jnp.log(u))
    return jnp.where(x > -0.5, near, um1)


def _gelu_tanh(x):
    c = 0.7978845608028654
    return 0.5 * x * (1.0 + jnp.tanh(c * (x + 0.044715 * (x * x * x))))


def _inproj_body(head_ref, x_ref, g_ref, b_ref, w_ref, q_ref, k_ref, v_ref, xr_ref, gr_ref,
                 *, n_pad, lead_steps, tm, aw, lw, q_scale):
    i = pl.program_id(1)
    rows_in = jnp.where(i < lead_steps, head_ref[...], x_ref[0])
    h = _layer_norm(rows_in, g_ref[...], b_ref[...])
    row = i * tm + lax.broadcasted_iota(jnp.int32, h.shape, 0)
    hb = jnp.where(row >= n_pad, h, 0.0).astype(BF16)

    def proj(lo, width):
        return jnp.dot(hb, w_ref[:, lo:lo + width], preferred_element_type=F32)

    q_ref[0] = (proj(0, aw) * q_scale).astype(BF16)
    k_ref[0] = proj(aw, aw).astype(BF16)
    v_ref[0] = proj(2 * aw, aw).astype(BF16)
    xr_ref[0] = proj(3 * aw, lw)
    gr_ref[0] = proj(3 * aw + lw, lw)


def _inproj(head, x, g, b, w_bf, *, n_pad, tm, aw, lw, q_scale):
    bsz, seq, d = x.shape
    assert head.shape[0] % tm == 0 and seq % tm == 0
    lead_steps = head.shape[0] // tm
    tp = head.shape[0] + seq
    cols = w_bf.shape[1]
    body = functools.partial(_inproj_body, n_pad=n_pad, lead_steps=lead_steps, tm=tm, aw=aw, lw=lw, q_scale=q_scale)
    row_spec = lambda width: pl.BlockSpec((1, tm, width), lambda bi, i: (bi, i, 0))
    return pl.pallas_call(
        body,
        grid=(bsz, tp // tm),
        in_specs=[
            pl.BlockSpec((tm, d), lambda bi, i: (jnp.minimum(i, lead_steps - 1), 0)),
            pl.BlockSpec((1, tm, d), lambda bi, i: (bi, jnp.maximum(i - lead_steps, 0), 0)),
            pl.BlockSpec((1, d), lambda bi, i: (0, 0)),
            pl.BlockSpec((1, d), lambda bi, i: (0, 0)),
            pl.BlockSpec((d, cols), lambda bi, i: (0, 0)),
        ],
        out_specs=[row_spec(aw), row_spec(aw), row_spec(aw), row_spec(lw), row_spec(lw)],
        out_shape=[
            jax.ShapeDtypeStruct((bsz, tp, aw), BF16),
            jax.ShapeDtypeStruct((bsz, tp, aw), BF16),
            jax.ShapeDtypeStruct((bsz, tp, aw), BF16),
            jax.ShapeDtypeStruct((bsz, tp, lw), F32),
            jax.ShapeDtypeStruct((bsz, tp, lw), F32),
        ],
        compiler_params=pltpu.CompilerParams(
            dimension_semantics=("arbitrary", "arbitrary"), vmem_limit_bytes=VMEM_LIMIT),
        name="inproj",
    )(head, x, g, b, w_bf)


UNDERFLOW_SUM = 105.0


def _attn_body(q_ref, k_ref, v_ref, tri_ref, o_ref, acc_scr, run_scr, *, dh, qb):
    g = pl.program_id(2)
    base = (g + 1) * qb
    tri = tri_ref[...]
    row = lax.broadcasted_iota(jnp.int32, (BLK, BLK), 0)
    col = lax.broadcasted_iota(jnp.int32, (BLK, BLK), 1)
    lane = lax.broadcasted_iota(jnp.int32, (BLK, 2 * dh), 1)
    chains = [(u, hh) for u in range(qb) for hh in range(2)]

    def q_head(u, hh):
        q = q_ref[0, u * BLK:(u + 1) * BLK, :]
        return jnp.where((lane >= hh * dh) & (lane < (hh + 1) * dh), q, jnp.zeros_like(q))

    def key_block(ref, j):
        return ref[0, pl.ds(pl.multiple_of(j * BLK, BLK), BLK), :]

    def sweep(o, diag):
        js = [base + u - o for u, _ in chains]
        zs = [lax.dot_general(q_head(u, hh), key_block(k_ref, j), (((1,), (1,)), ((), ())),
                              preferred_element_type=F32) for (u, hh), j in zip(chains, js)]
        sums = []
        for z in zs:
            sp = jnp.maximum(z, 0.0) + jnp.log(1.0 + jnp.exp(-jnp.abs(z)))
            if diag:
                sp = jnp.where(col < row, sp, 0.0)
            hi = sp.astype(BF16)
            lo = (sp - hi.astype(F32)).astype(BF16)
            sums.append(jnp.dot(jnp.concatenate([hi, lo], axis=1), tri, preferred_element_type=F32))
        run_min = None
        for c, (z, s, j) in enumerate(zip(zs, sums, js)):
            run = jnp.zeros((BLK, BLK), F32) if diag else run_scr[c]
            att = jnp.exp(z - s[:, :BLK] - run)
            if diag:
                att = jnp.where(col < row, att, 0.0)
            av = jnp.dot(att.astype(BF16), key_block(v_ref, j), preferred_element_type=F32)
            acc_scr[c] = av if diag else acc_scr[c] + av
            run = run + s[:, BLK:]
            run_scr[c] = run
            run_min = run if run_min is None else jnp.minimum(run_min, run)
        return (jnp.min(run_min) >= UNDERFLOW_SUM).astype(jnp.int32)

    def cond(carry):
        o, done = carry
        return (o <= base) & (done == 0)

    lax.while_loop(cond, lambda carry: (carry[0] + 1, sweep(carry[0], False)), (jnp.int32(1), sweep(0, True)))
    for u in range(qb):
        o_ref[0, u * BLK:(u + 1) * BLK, :] = jnp.where(lane < dh, acc_scr[2 * u], acc_scr[2 * u + 1])


def _attn(q, k, v, *, dh, qb):
    bsz, tp, aw = q.shape
    nq = tp // BLK - qb
    pair = 2 * dh
    ridx = jnp.arange(2 * BLK)[:, None] % BLK
    cidx = jnp.arange(2 * BLK)[None, :]
    tri = ((cidx >= BLK) | (ridx >= cidx)).astype(BF16)
    body = functools.partial(_attn_body, dh=dh, qb=qb)
    return pl.pallas_call(
        body,
        grid=(bsz, aw // pair, nq // qb),
        in_specs=[
            pl.BlockSpec((1, qb * BLK, pair), lambda b, hp, g: (b, g + 1, hp)),
            pl.BlockSpec((1, tp, pair), lambda b, hp, g: (b, 0, hp)),
            pl.BlockSpec((1, tp, pair), lambda b, hp, g: (b, 0, hp)),
            pl.BlockSpec((2 * BLK, 2 * BLK), lambda b, hp, g: (0, 0)),
        ],
        out_specs=pl.BlockSpec((1, qb * BLK, pair), lambda b, hp, g: (b, g, hp)),
        out_shape=jax.ShapeDtypeStruct((bsz, nq * BLK, aw), F32),
        scratch_shapes=[pltpu.VMEM((2 * qb, BLK, pair), F32), pltpu.VMEM((2 * qb, BLK, BLK), F32)],
        compiler_params=pltpu.CompilerParams(
            dimension_semantics=("arbitrary", "arbitrary", "arbitrary"), vmem_limit_bytes=VMEM_LIMIT),
        name="attn",
    )(q, k, v, tri)


def _lru_body(xr_ref, gr_ref, cw_ref, cb_ref, wg_ref, bg_ref, lam_ref, o_ref, xbuf, hprev, *, n_pad, lw):
    c = pl.program_id(1)

    @pl.when(c == 0)
    def _():
        xbuf[0:SUBLANES, :] = jnp.zeros((SUBLANES, lw), F32)
        hprev[...] = jnp.zeros_like(hprev)

    x = xr_ref[0]
    xbuf[SUBLANES:SUBLANES + BLK, :] = x
    xc = cw_ref[CONV_WIDTH - 1:CONV_WIDTH, :] * x + cb_ref[...]
    for tap in range(1, CONV_WIDTH):
        xs = xbuf[SUBLANES - tap:SUBLANES - tap + BLK, :]
        xc = xc + cw_ref[CONV_WIDTH - 1 - tap:CONV_WIDTH - tap, :] * xs
    xbuf[0:SUBLANES, :] = x[BLK - SUBLANES:, :]

    gates = jnp.dot(xc.astype(BF16), wg_ref[...], preferred_element_type=F32) + bg_ref[...]
    r = jax.nn.sigmoid(gates[:, :lw])
    ig = jax.nn.sigmoid(gates[:, lw:])
    log_a = (-LRU_C) * r * _softplus(-lam_ref[...])
    a = jnp.exp(log_a)
    b = jnp.sqrt(-_expm1(2.0 * log_a)) * (ig * xc)
    row = lax.broadcasted_iota(jnp.int32, (BLK, lw), 0)
    b = jnp.where(c * BLK + row >= n_pad, b, 0.0)

    d = 1
    while d < BLK:
        a_s = pltpu.roll(a, d, axis=0)
        b_s = pltpu.roll(b, d, axis=0)
        m = row >= d
        b = jnp.where(m, a * b_s + b, b)
        a = jnp.where(m, a * a_s, a)
        d *= 2
    h = a * hprev[...] + b
    hprev[...] = h[BLK - 1:BLK, :]
    o_ref[0] = h * _gelu_tanh(gr_ref[0])


def _lru(xr, gr, cw, cb, wg_bf, bg, lam, *, n_pad, lead):
    bsz, tp, lw = xr.shape
    nblk = tp // BLK
    body = functools.partial(_lru_body, n_pad=n_pad, lw=lw)
    vec = lambda rows, width: pl.BlockSpec((rows, width), lambda b, c: (0, 0))
    return pl.pallas_call(
        body,
        grid=(bsz, nblk),
        in_specs=[
            pl.BlockSpec((1, BLK, lw), lambda b, c: (b, c, 0)),
            pl.BlockSpec((1, BLK, lw), lambda b, c: (b, c, 0)),
            vec(CONV_WIDTH, lw), vec(1, lw), vec(lw, 2 * lw), vec(1, 2 * lw), vec(1, lw),
        ],
        out_specs=pl.BlockSpec((1, BLK, lw), lambda b, c: (b, jnp.maximum(c - lead, 0), 0)),
        out_shape=jax.ShapeDtypeStruct((bsz, tp - lead * BLK, lw), F32),
        scratch_shapes=[pltpu.VMEM((SUBLANES + BLK, lw), F32), pltpu.VMEM((1, lw), F32)],
        compiler_params=pltpu.CompilerParams(
            dimension_semantics=("arbitrary", "arbitrary"), vmem_limit_bytes=VMEM_LIMIT),
        name="lru",
    )(xr, gr, cw, cb, wg_bf, bg, lam)


def _outproj_body(attn_ref, lru_ref, x_ref, ga_ref, gl_ref, wo_ref, ge_ref, be_ref, g1_ref, b1_ref, o_ref,
                  *, alpha, aw):
    na = _rms_norm(attn_ref[0], ga_ref[...]).astype(BF16)
    nl = _rms_norm(lru_ref[0], gl_ref[...]).astype(BF16)
    mixed = jnp.dot(na, wo_ref[:aw, :], preferred_element_type=F32)
    mixed = mixed + jnp.dot(nl, wo_ref[aw:, :], preferred_element_type=F32)
    h = _layer_norm(x_ref[0], ge_ref[...], be_ref[...])
    h1 = _layer_norm(alpha * h + mixed, g1_ref[...], b1_ref[...])
    o_ref[...] = h1.T


def _outproj(attn, lru, x, ga, gl, wo_bf, ge, be, g1, b1, *, alpha, tm):
    bsz, seq, d = x.shape
    aw = attn.shape[-1]
    lw = lru.shape[-1]
    nt = seq // tm
    body = functools.partial(_outproj_body, alpha=alpha, aw=aw)
    vec = lambda width: pl.BlockSpec((1, width), lambda b, i: (0, 0))
    return pl.pallas_call(
        body,
        grid=(bsz, nt),
        in_specs=[
            pl.BlockSpec((1, tm, aw), lambda b, i: (b, i, 0)),
            pl.BlockSpec((1, tm, lw), lambda b, i: (b, i, 0)),
            pl.BlockSpec((1, tm, d), lambda b, i: (b, i, 0)),
            vec(aw), vec(lw),
            pl.BlockSpec((aw + lw, d), lambda b, i: (0, 0)),
            vec(d), vec(d), vec(d), vec(d),
        ],
        out_specs=pl.BlockSpec((d, tm), lambda b, i: (0, b * nt + i)),
        out_shape=jax.ShapeDtypeStruct((d, bsz * seq), F32),
        compiler_params=pltpu.CompilerParams(
            dimension_semantics=("arbitrary", "arbitrary"), vmem_limit_bytes=VMEM_LIMIT),
        name="outproj",
    )(attn, lru, x, ga, gl, wo_bf, ge, be, g1, b1)


def _top_k_ranks(s, k, exact):
    rows, n = s.shape
    row_id = lax.broadcasted_iota(jnp.int32, (rows, n), 0)
    vals = []
    rank = jnp.full((rows, n), float(k), F32)
    for r in range(k):
        m = jnp.max(s, axis=0, keepdims=True)
        hit = s == m
        if exact:
            first = jnp.min(jnp.where(hit, row_id, rows), axis=0, keepdims=True)
            hit = row_id == first
        vals.append(m)
        rank = jnp.where(hit, float(r), rank)
        s = jnp.where(hit, -jnp.inf, s)
    return jnp.concatenate(vals, axis=0), rank


def _peer_route_head(h, xb, wq_ref, sk_ref, rank2_scr, nsel_scr, e1_scr, e2_scr, *, half, exact):
    k = PEER_TOPK
    qo = pl.multiple_of(h * 2 * half, 2 * half)
    qh = jnp.dot(wq_ref[pl.ds(qo, 2 * half), :], xb, preferred_element_type=F32)
    s1_all = jnp.dot(sk_ref[h, 0], qh[:half].astype(BF16), preferred_element_type=F32)
    s2_all = jnp.dot(sk_ref[h, 1], qh[half:].astype(BF16), preferred_element_type=F32)
    widths = [k // (r1 + 1) for r1 in range(k)]
    pad = -sum(widths) % SUBLANES
    count = lambda hit: jnp.sum(jnp.where(hit, 1.0, 0.0), axis=0, keepdims=True)
    ok = None
    for lt in range(s1_all.shape[1] // LANES):
        ls = slice(lt * LANES, (lt + 1) * LANES)
        s1, s2 = s1_all[:, ls], s2_all[:, ls]
        v1, rank1 = _top_k_ranks(s1, k, exact)
        v2, rank2 = _top_k_ranks(s2, k, exact)
        cand = jnp.concatenate([v1[r1:r1 + 1] + v2[:w] for r1, w in enumerate(widths)]
                               + [jnp.full((pad, LANES), -jnp.inf, F32)], axis=0)
        _, rank_c = _top_k_ranks(cand, k, exact)
        picked = rank_c < k
        zsum = jnp.sum(jnp.where(picked, jnp.exp(cand - (v1[0:1] + v2[0:1])), 0.0), axis=0, keepdims=True)
        nsel = jnp.zeros_like(s1)
        off = 0
        for r1, w in enumerate(widths):
            nsel = jnp.where(rank1 == float(r1), count(picked[off:off + w]), nsel)
            off += w
        rank2_scr[h, :, ls] = rank2.astype(BF16)
        nsel_scr[h, :, :, ls] = nsel.reshape(nsel_scr.shape[1:3] + (LANES,))
        e1_scr[h, :, :, ls] = jnp.exp(s1 - v1[0:1]).reshape(e1_scr.shape[1:3] + (LANES,))
        e2_scr[h, :, ls] = (jnp.exp(s2 - v2[0:1]) / zsum).astype(BF16)
        ok_lt = (count(rank1 < k) == k) & (count(rank2 < k) == k) & (count(picked) == k)
        ok = ok_lt if ok is None else ok & ok_lt
    return jnp.min(jnp.where(ok, 1, 0))


def _peer_body(xt_ref, wq_ref, sk_ref, d_ref, ut_ref, g2_ref, b2_ref, o_ref,
               xb_scr, rank2_scr, nsel_scr, e1_scr, e2_scr, a_scr, acc_scr,
               *, alpha, nk, half, tm, cpg):
    j = pl.program_id(1)
    nlt = tm // LANES
    rows = 2 * SUBLANES

    @pl.when(j == 0)
    def _route():
        xb = xt_ref[...].astype(BF16)
        xb_scr[...] = xb
        acc_scr[...] = jnp.zeros_like(acc_scr)
        tables = (rank2_scr, nsel_scr, e1_scr, e2_scr)

        def route_all(exact):
            return lax.fori_loop(0, PEER_HEADS, lambda h, ok: jnp.minimum(ok, _peer_route_head(
                h, xb, wq_ref, sk_ref, *tables, half=half, exact=exact)), jnp.int32(1))

        unambiguous = route_all(False)

        @pl.when(unambiguous == 0)
        def _():
            route_all(True)

    xb = xb_scr[...]
    pieces = cpg // 2
    er = 2 * nk

    def row_bf16(tab, h, cc, ls):
        return jnp.broadcast_to(tab[h, j, cc:cc + 1, ls], (rows, LANES)).astype(BF16)

    p_next = jnp.dot(d_ref[0:er, :], xb, preferred_element_type=F32)
    for m in range(pieces):
        p_cur = p_next
        if m + 1 < pieces:
            p_next = jnp.dot(d_ref[(m + 1) * er:(m + 2) * er, :], xb, preferred_element_type=F32)
        for lt in range(nlt):
            ls = slice(lt * LANES, (lt + 1) * LANES)
            nrg = nk // rows
            g = [[jnp.zeros((rows, LANES), BF16) for _ in range(2)] for _ in range(nrg)]
            for h in range(PEER_HEADS):
                nrow = [row_bf16(nsel_scr, h, 2 * m + c2, ls) for c2 in range(2)]
                erow = [row_bf16(e1_scr, h, 2 * m + c2, ls) for c2 in range(2)]
                for rg in range(nrg):
                    r2 = rank2_scr[h, rg * rows:(rg + 1) * rows, ls]
                    e2 = e2_scr[h, rg * rows:(rg + 1) * rows, ls]
                    for c2 in range(2):
                        w = jnp.where(r2 < nrow[c2], e2, jnp.zeros_like(e2))
                        g[rg][c2] = g[rg][c2] + w * erow[c2]
            for rg in range(nrg):
                for c2 in range(2):
                    act = _gelu_tanh(p_cur[c2 * nk + rg * rows:c2 * nk + (rg + 1) * rows, ls])
                    a0 = (2 * m + c2) * nk + rg * rows
                    a_scr[a0:a0 + rows, ls] = g[rg][c2] * act.astype(BF16)
        if m % 2 == 1:
            e0 = (m - 1) * er
            acc_scr[...] += jnp.dot(ut_ref[:, e0:e0 + 2 * er], a_scr[e0:e0 + 2 * er, :], preferred_element_type=F32)

    @pl.when(j == pl.num_programs(1) - 1)
    def _finish():
        y = (alpha * xt_ref[...] + acc_scr[...]).T
        o_ref[...] = _layer_norm(y, g2_ref[...], b2_ref[...])


def _peer(h1t, wq_bf, sk_bf, down_bf, upt_bf, g2, b2, *, alpha, tm, cpg):
    d, n = h1t.shape
    nk, half = sk_bf.shape[2], sk_bf.shape[3]
    n_exp = down_bf.shape[0]
    assert cpg == SUBLANES and nk % (2 * SUBLANES) == 0
    eg = cpg * nk
    ng = n_exp // eg
    body = functools.partial(_peer_body, alpha=alpha, nk=nk, half=half, tm=tm, cpg=cpg)
    head_scr = lambda: pltpu.VMEM((PEER_HEADS, nk, tm), BF16)
    group_scr = lambda: pltpu.VMEM((PEER_HEADS, nk // cpg, cpg, tm), F32)
    return pl.pallas_call(
        body,
        grid=(n // tm, ng),
        in_specs=[
            pl.BlockSpec((d, tm), lambda i, j: (0, i), pipeline_mode=pl.Buffered(1)),
            pl.BlockSpec(wq_bf.shape, lambda i, j: (0, 0), pipeline_mode=pl.Buffered(1)),
            pl.BlockSpec(sk_bf.shape, lambda i, j: (0, 0, 0, 0), pipeline_mode=pl.Buffered(1)),
            pl.BlockSpec((eg, d), lambda i, j: (j, 0)),
            pl.BlockSpec((d, eg), lambda i, j: (0, j)),
            pl.BlockSpec((1, d), lambda i, j: (0, 0), pipeline_mode=pl.Buffered(1)),
            pl.BlockSpec((1, d), lambda i, j: (0, 0), pipeline_mode=pl.Buffered(1)),
        ],
        out_specs=pl.BlockSpec((tm, d), lambda i, j: (i, 0)),
        out_shape=jax.ShapeDtypeStruct((n, d), F32),
        scratch_shapes=[
            pltpu.VMEM((d, tm), BF16),
            head_scr(), group_scr(), group_scr(), head_scr(),
            pltpu.VMEM((eg, tm), BF16),
            pltpu.VMEM((d, tm), F32),
        ],
        compiler_params=pltpu.CompilerParams(
            dimension_semantics=("arbitrary", "arbitrary"), vmem_limit_bytes=VMEM_LIMIT),
        name="peer",
    )(h1t, wq_bf, sk_bf, down_bf, upt_bf, g2, b2)


def _pick_tile(n, candidates):
    for t in candidates:
        if n % t == 0:
            return t
    raise ValueError(f"no tile for {n}")


def _block_diag(w):
    nh, di, dj = w.shape
    eye = jnp.eye(nh, dtype=w.dtype)
    return (eye[:, None, :, None] * w[:, :, None, :]).reshape(nh * di, nh * dj)


def kernel(x, meta_tokens, ln_emb_g, ln_emb_b, w_in, conv_w, conv_b, gate_a_w, gate_a_b, gate_x_w, gate_x_b,
           lru_lambda, attn_norm_g, lru_norm_g, w_out, ln1_g, ln1_b, peer_query_w, peer_sub_keys, peer_down,
           peer_up, ln2_g, ln2_b):
    bsz, seq, d = x.shape
    depth = w_in.shape[0]
    assert depth == 1, "single-layer stack only"
    n_meta = meta_tokens.shape[0]
    aw = attn_norm_g.shape[-1]
    lw = lru_norm_g.shape[-1]
    dh = aw // ATTN_HEADS
    lead = ATTN_QB
    assert 2 * dh == LANES and seq % (ATTN_QB * BLK) == 0 and n_meta <= BLK
    n_pad = lead * BLK - n_meta
    alpha = (2.0 * depth) ** 0.25
    row = lambda v: v.reshape(1, -1)

    head = jnp.concatenate([jnp.zeros((n_pad, d), x.dtype), meta_tokens.astype(x.dtype)], axis=0)

    q, k, v, xr, gr = _inproj(
        head, x, row(ln_emb_g), row(ln_emb_b), w_in[0].astype(BF16),
        n_pad=n_pad, tm=_pick_tile(lead * BLK, (512, 256, 128)), aw=aw, lw=lw, q_scale=dh ** -0.5)

    attn = _attn(q, k, v, dh=dh, qb=ATTN_QB)

    wg = jnp.concatenate([_block_diag(gate_a_w[0]), _block_diag(gate_x_w[0])], axis=1).astype(BF16)
    bg = jnp.concatenate([gate_a_b[0], gate_x_b[0]]).reshape(1, -1)
    lru = _lru(xr, gr, conv_w[0], row(conv_b[0]), wg, bg, row(lru_lambda[0]), n_pad=n_pad, lead=lead)

    h1t = _outproj(attn, lru, x, row(attn_norm_g[0]), row(lru_norm_g[0]), w_out[0].astype(BF16),
                   row(ln_emb_g), row(ln_emb_b), row(ln1_g[0]), row(ln1_b[0]),
                   alpha=alpha, tm=_pick_tile(seq, (512, 256, 128)))

    nk = peer_sub_keys.shape[3]
    wq = peer_query_w[0].reshape(d, -1).T.astype(BF16)
    out = _peer(h1t, wq, peer_sub_keys[0].astype(BF16), peer_down[0].astype(BF16),
                peer_up[0].T.astype(BF16), row(ln2_g[0]), row(ln2_b[0]),
                alpha=alpha, tm=_pick_tile(bsz * seq, (1024, 512, 256, 128)), cpg=SUBLANES)
    return out.reshape(bsz, seq, d)
```

```python
import functools

import jax
import jax.numpy as jnp
from jax import lax
from jax.experimental import pallas as pl
from jax.experimental.pallas import tpu as pltpu

F32 = jnp.float32
BF16 = jnp.bfloat16

BLK = 128
LANES = 128
SUBLANES = 8
ATTN_HEADS = 8
ATTN_QB = 8
LRU_HEADS = 8
CONV_WIDTH = 4
LRU_C = 8.0
PEER_HEADS = 8
PEER_TOPK = 16
NORM_EPS = 1e-5
VMEM_LIMIT = 56 * 1024 * 1024


def _layer_norm(x, g, b):
    mu = jnp.mean(x, axis=-1, keepdims=True)
    xc = x - mu
    var = jnp.mean(xc * xc, axis=-1, keepdims=True)
    return xc * lax.rsqrt(var + NORM_EPS) * g + b


def _rms_norm(x, g):
    ms = jnp.mean(x * x, axis=-1, keepdims=True)
    return x * lax.rsqrt(ms + NORM_EPS) * g


def _softplus(z):
    return jnp.maximum(z, 0.0) + jnp.log1p(jnp.exp(-jnp.abs(z)))


def _expm1(x):
    u = jnp.exp(x)
    um1 = u - 1.0
    near = jnp.where(u == 1.0, x, um1 * x / jnp.log(u))
    return jnp.where(x > -0.5, near, um1)


def _gelu_tanh(x):
    c = 0.7978845608028654
    return 0.5 * x * (1.0 + jnp.tanh(c * (x + 0.044715 * (x * x * x))))


def _inproj_body(head_ref, x_ref, g_ref, b_ref, w_ref, q_ref, k_ref, v_ref, xr_ref, gr_ref,
                 *, n_pad, lead_steps, tm, aw, lw, q_scale):
    i = pl.program_id(1)
    rows_in = jnp.where(i < lead_steps, head_ref[...], x_ref[0])
    h = _layer_norm(rows_in, g_ref[...], b_ref[...])
    row = i * tm + lax.broadcasted_iota(jnp.int32, h.shape, 0)
    hb = jnp.where(row >= n_pad, h, 0.0).astype(BF16)

    def proj(lo, width):
        return jnp.dot(hb, w_ref[:, lo:lo + width], preferred_element_type=F32)

    q_ref[0] = (proj(0, aw) * q_scale).astype(BF16)
    k_ref[0] = proj(aw, aw).astype(BF16)
    v_ref[0] = proj(2 * aw, aw).astype(BF16)
    xr_ref[0] = proj(3 * aw, lw)
    gr_ref[0] = proj(3 * aw + lw, lw)


def _inproj(head, x, g, b, w_bf, *, n_pad, tm, aw, lw, q_scale):
    bsz, seq, d = x.shape
    assert head.shape[0] % tm == 0 and seq % tm == 0
    lead_steps = head.shape[0] // tm
    tp = head.shape[0] + seq
    cols = w_bf.shape[1]
    body = functools.partial(_inproj_body, n_pad=n_pad, lead_steps=lead_steps, tm=tm, aw=aw, lw=lw, q_scale=q_scale)
    row_spec = lambda width: pl.BlockSpec((1, tm, width), lambda bi, i: (bi, i, 0))
    return pl.pallas_call(
        body,
        grid=(bsz, tp // tm),
        in_specs=[
            pl.BlockSpec((tm, d), lambda bi, i: (jnp.minimum(i, lead_steps - 1), 0)),
            pl.BlockSpec((1, tm, d), lambda bi, i: (bi, jnp.maximum(i - lead_steps, 0), 0)),
            pl.BlockSpec((1, d), lambda bi, i: (0, 0)),
            pl.BlockSpec((1, d), lambda bi, i: (0, 0)),
            pl.BlockSpec((d, cols), lambda bi, i: (0, 0)),
        ],
        out_specs=[row_spec(aw), row_spec(aw), row_spec(aw), row_spec(lw), row_spec(lw)],
        out_shape=[
            jax.ShapeDtypeStruct((bsz, tp, aw), BF16),
            jax.ShapeDtypeStruct((bsz, tp, aw), BF16),
            jax.ShapeDtypeStruct((bsz, tp, aw), BF16),
            jax.ShapeDtypeStruct((bsz, tp, lw), F32),
            jax.ShapeDtypeStruct((bsz, tp, lw), F32),
        ],
        compiler_params=pltpu.CompilerParams(
            dimension_semantics=("arbitrary", "arbitrary"), vmem_limit_bytes=VMEM_LIMIT),
        name="inproj",
    )(head, x, g, b, w_bf)


UNDERFLOW_SUM = 105.0


def _attn_body(q_ref, k_ref, v_ref, tri_ref, o_ref, acc_scr, run_scr, *, dh, qb):
    g = pl.program_id(2)
    base = (g + 1) * qb
    tri = tri_ref[...]
    row = lax.broadcasted_iota(jnp.int32, (BLK, BLK), 0)
    col = lax.broadcasted_iota(jnp.int32, (BLK, BLK), 1)
    lane = lax.broadcasted_iota(jnp.int32, (BLK, 2 * dh), 1)
    chains = [(u, hh) for u in range(qb) for hh in range(2)]

    def q_head(u, hh):
        q = q_ref[0, u * BLK:(u + 1) * BLK, :]
        return jnp.where((lane >= hh * dh) & (lane < (hh + 1) * dh), q, jnp.zeros_like(q))

    def key_block(ref, j):
        return ref[0, pl.ds(pl.multiple_of(j * BLK, BLK), BLK), :]

    def sweep(o, diag):
        js = [base + u - o for u, _ in chains]
        zs = [lax.dot_general(q_head(u, hh), key_block(k_ref, j), (((1,), (1,)), ((), ())),
                              preferred_element_type=F32) for (u, hh), j in zip(chains, js)]
        sums = []
        for z in zs:
            sp = jnp.maximum(z, 0.0) + jnp.log(1.0 + jnp.exp(-jnp.abs(z)))
            if diag:
                sp = jnp.where(col < row, sp, 0.0)
            hi = sp.astype(BF16)
            lo = (sp - hi.astype(F32)).astype(BF16)
            sums.append(jnp.dot(jnp.concatenate([hi, lo], axis=1), tri, preferred_element_type=F32))
        run_min = None
        for c, (z, s, j) in enumerate(zip(zs, sums, js)):
            run = jnp.zeros((BLK, BLK), F32) if diag else run_scr[c]
            att = jnp.exp(z - s[:, :BLK] - run)
            if diag:
                att = jnp.where(col < row, att, 0.0)
            av = jnp.dot(att.astype(BF16), key_block(v_ref, j), preferred_element_type=F32)
            acc_scr[c] = av if diag else acc_scr[c] + av
            run = run + s[:, BLK:]
            run_scr[c] = run
            run_min = run if run_min is None else jnp.minimum(run_min, run)
        return (jnp.min(run_min) >= UNDERFLOW_SUM).astype(jnp.int32)

    def cond(carry):
        o, done = carry
        return (o <= base) & (done == 0)

    lax.while_loop(cond, lambda carry: (carry[0] + 1, sweep(carry[0], False)), (jnp.int32(1), sweep(0, True)))
    for u in range(qb):
        o_ref[0, u * BLK:(u + 1) * BLK, :] = jnp.where(lane < dh, acc_scr[2 * u], acc_scr[2 * u + 1])


def _attn(q, k, v, *, dh, qb):
    bsz, tp, aw = q.shape
    nq = tp // BLK - qb
    pair = 2 * dh
    ridx = jnp.arange(2 * BLK)[:, None] % BLK
    cidx = jnp.arange(2 * BLK)[None, :]
    tri = ((cidx >= BLK) | (ridx >= cidx)).astype(BF16)
    body = functools.partial(_attn_body, dh=dh, qb=qb)
    return pl.pallas_call(
        body,
        grid=(bsz, aw // pair, nq // qb),
        in_specs=[
            pl.BlockSpec((1, qb * BLK, pair), lambda b, hp, g: (b, g + 1, hp)),
            pl.BlockSpec((1, tp, pair), lambda b, hp, g: (b, 0, hp)),
            pl.BlockSpec((1, tp, pair), lambda b, hp, g: (b, 0, hp)),
            pl.BlockSpec((2 * BLK, 2 * BLK), lambda b, hp, g: (0, 0)),
        ],
        out_specs=pl.BlockSpec((1, qb * BLK, pair), lambda b, hp, g: (b, g, hp)),
        out_shape=jax.ShapeDtypeStruct((bsz, nq * BLK, aw), F32),
        scratch_shapes=[pltpu.VMEM((2 * qb, BLK, pair), F32), pltpu.VMEM((2 * qb, BLK, BLK), F32)],
        compiler_params=pltpu.CompilerParams(
            dimension_semantics=("arbitrary", "arbitrary", "arbitrary"), vmem_limit_bytes=VMEM_LIMIT),
        name="attn",
    )(q, k, v, tri)


def _lru_body(xr_ref, gr_ref, cw_ref, cb_ref, wg_ref, bg_ref, lam_ref, o_ref, xbuf, hprev, *, n_pad, lw):
    c = pl.program_id(0)
    bsz = xr_ref.shape[0]

    @pl.when(c == 0)
    def _():
        xbuf[:, 0:SUBLANES, :] = jnp.zeros((bsz, SUBLANES, lw), F32)
        hprev[...] = jnp.zeros_like(hprev)

    xcs = []
    for bi in range(bsz):
        x = xr_ref[bi]
        xbuf[bi, SUBLANES:SUBLANES + BLK, :] = x
        xc = cw_ref[CONV_WIDTH - 1:CONV_WIDTH, :] * x + cb_ref[...]
        for tap in range(1, CONV_WIDTH):
            xs = xbuf[bi, SUBLANES - tap:SUBLANES - tap + BLK, :]
            xc = xc + cw_ref[CONV_WIDTH - 1 - tap:CONV_WIDTH - tap, :] * xs
        xbuf[bi, 0:SUBLANES, :] = x[BLK - SUBLANES:, :]
        xcs.append(xc)

    gates_all = jnp.dot(jnp.concatenate(xcs, axis=0).astype(BF16), wg_ref[...],
                        preferred_element_type=F32) + bg_ref[...]
    decay = (-LRU_C) * _softplus(-lam_ref[...])
    row = lax.broadcasted_iota(jnp.int32, (BLK, lw), 0)
    for bi in range(bsz):
        gates, xc = gates_all[bi * BLK:(bi + 1) * BLK], xcs[bi]
        r = jax.nn.sigmoid(gates[:, :lw])
        ig = jax.nn.sigmoid(gates[:, lw:])
        log_a = decay * r
        a = jnp.exp(log_a)
        b = jnp.sqrt(-_expm1(2.0 * log_a)) * (ig * xc)
        b = jnp.where(c * BLK + row >= n_pad, b, 0.0)
        d = 1
        while d < BLK:
            a_s = pltpu.roll(a, d, axis=0)
            b_s = pltpu.roll(b, d, axis=0)
            m = row >= d
            b = jnp.where(m, a * b_s + b, b)
            a = jnp.where(m, a * a_s, a)
            d *= 2
        h = a * hprev[bi] + b
        hprev[bi] = h[BLK - 1:BLK, :]
        o_ref[bi] = h * _gelu_tanh(gr_ref[bi])


def _lru(xr, gr, cw, cb, wg_bf, bg, lam, *, n_pad, lead):
    bsz, tp, lw = xr.shape
    nblk = tp // BLK
    body = functools.partial(_lru_body, n_pad=n_pad, lw=lw)
    vec = lambda rows, width: pl.BlockSpec((rows, width), lambda c: (0, 0))
    return pl.pallas_call(
        body,
        grid=(nblk,),
        in_specs=[
            pl.BlockSpec((bsz, BLK, lw), lambda c: (0, c, 0)),
            pl.BlockSpec((bsz, BLK, lw), lambda c: (0, c, 0)),
            vec(CONV_WIDTH, lw), vec(1, lw), vec(lw, 2 * lw), vec(1, 2 * lw), vec(1, lw),
        ],
        out_specs=pl.BlockSpec((bsz, BLK, lw), lambda c: (0, jnp.maximum(c - lead, 0), 0)),
        out_shape=jax.ShapeDtypeStruct((bsz, tp - lead * BLK, lw), F32),
        scratch_shapes=[pltpu.VMEM((bsz, SUBLANES + BLK, lw), F32), pltpu.VMEM((bsz, 1, lw), F32)],
        compiler_params=pltpu.CompilerParams(
            dimension_semantics=("arbitrary",), vmem_limit_bytes=VMEM_LIMIT),
        name="lru",
    )(xr, gr, cw, cb, wg_bf, bg, lam)


def _outproj_body(attn_ref, lru_ref, x_ref, ga_ref, gl_ref, wo_ref, ge_ref, be_ref, g1_ref, b1_ref, o_ref,
                  *, alpha, aw):
    na = _rms_norm(attn_ref[0], ga_ref[...]).astype(BF16)
    nl = _rms_norm(lru_ref[0], gl_ref[...]).astype(BF16)
    mixed = jnp.dot(na, wo_ref[:aw, :], preferred_element_type=F32)
    mixed = mixed + jnp.dot(nl, wo_ref[aw:, :], preferred_element_type=F32)
    h = _layer_norm(x_ref[0], ge_ref[...], be_ref[...])
    h1 = _layer_norm(alpha * h + mixed, g1_ref[...], b1_ref[...])
    o_ref[...] = h1.T


def _outproj(attn, lru, x, ga, gl, wo_bf, ge, be, g1, b1, *, alpha, tm):
    bsz, seq, d = x.shape
    aw = attn.shape[-1]
    lw = lru.shape[-1]
    nt = seq // tm
    body = functools.partial(_outproj_body, alpha=alpha, aw=aw)
    vec = lambda width: pl.BlockSpec((1, width), lambda b, i: (0, 0))
    return pl.pallas_call(
        body,
        grid=(bsz, nt),
        in_specs=[
            pl.BlockSpec((1, tm, aw), lambda b, i: (b, i, 0)),
            pl.BlockSpec((1, tm, lw), lambda b, i: (b, i, 0)),
            pl.BlockSpec((1, tm, d), lambda b, i: (b, i, 0)),
            vec(aw), vec(lw),
            pl.BlockSpec((aw + lw, d), lambda b, i: (0, 0)),
            vec(d), vec(d), vec(d), vec(d),
        ],
        out_specs=pl.BlockSpec((d, tm), lambda b, i: (0, b * nt + i)),
        out_shape=jax.ShapeDtypeStruct((d, bsz * seq), F32),
        compiler_params=pltpu.CompilerParams(
            dimension_semantics=("arbitrary", "arbitrary"), vmem_limit_bytes=VMEM_LIMIT),
        name="outproj",
    )(attn, lru, x, ga, gl, wo_bf, ge, be, g1, b1)


def _top_k_ranks(s, k, exact):
    rows, n = s.shape
    row_id = lax.broadcasted_iota(jnp.int32, (rows, n), 0)
    vals = []
    rank = jnp.full((rows, n), float(k), F32)
    for r in range(k):
        m = jnp.max(s, axis=0, keepdims=True)
        hit = s == m
        if exact:
            first = jnp.min(jnp.where(hit, row_id, rows), axis=0, keepdims=True)
            hit = row_id == first
        vals.append(m)
        rank = jnp.where(hit, float(r), rank)
        s = jnp.where(hit, -jnp.inf, s)
    return jnp.concatenate(vals, axis=0), rank


def _peer_route_head(h, xb, wq_ref, sk_ref, rank2_scr, nsel_scr, e1_scr, e2_scr, *, half, exact):
    k = PEER_TOPK
    qo = pl.multiple_of(h * 2 * half, 2 * half)
    qh = jnp.dot(wq_ref[pl.ds(qo, 2 * half), :], xb, preferred_element_type=F32)
    s1_all = jnp.dot(sk_ref[h, 0], qh[:half].astype(BF16), preferred_element_type=F32)
    s2_all = jnp.dot(sk_ref[h, 1], qh[half:].astype(BF16), preferred_element_type=F32)
    widths = [k // (r1 + 1) for r1 in range(k)]
    pad = -sum(widths) % SUBLANES
    count = lambda hit: jnp.sum(jnp.where(hit, 1.0, 0.0), axis=0, keepdims=True)
    ok = None
    for lt in range(s1_all.shape[1] // LANES):
        ls = slice(lt * LANES, (lt + 1) * LANES)
        s1, s2 = s1_all[:, ls], s2_all[:, ls]
        v1, rank1 = _top_k_ranks(s1, k, exact)
        v2, rank2 = _top_k_ranks(s2, k, exact)
        cand = jnp.concatenate([v1[r1:r1 + 1] + v2[:w] for r1, w in enumerate(widths)]
                               + [jnp.full((pad, LANES), -jnp.inf, F32)], axis=0)
        _, rank_c = _top_k_ranks(cand, k, exact)
        picked = rank_c < k
        zsum = jnp.sum(jnp.where(picked, jnp.exp(cand - (v1[0:1] + v2[0:1])), 0.0), axis=0, keepdims=True)
        nsel = jnp.zeros_like(s1)
        off = 0
        for r1, w in enumerate(widths):
            nsel = jnp.where(rank1 == float(r1), count(picked[off:off + w]), nsel)
            off += w
        rank2_scr[h, :, ls] = rank2.astype(BF16)
        nsel_scr[h, :, :, ls] = nsel.reshape(nsel_scr.shape[1:3] + (LANES,))
        e1_scr[h, :, :, ls] = jnp.exp(s1 - v1[0:1]).reshape(e1_scr.shape[1:3] + (LANES,))
        e2_scr[h, :, ls] = (jnp.exp(s2 - v2[0:1]) / zsum).astype(BF16)
        ok_lt = (count(rank1 < k) == k) & (count(rank2 < k) == k) & (count(picked) == k)
        ok = ok_lt if ok is None else ok & ok_lt
    return jnp.min(jnp.where(ok, 1, 0))


def _peer_body(xt_ref, wq_ref, sk_ref, d_ref, ut_ref, g2_ref, b2_ref, o_ref,
               xb_scr, rank2_scr, nsel_scr, e1_scr, e2_scr, a_scr, acc_scr,
               *, alpha, nk, half, tm, cpg):
    j = pl.program_id(1)
    nlt = tm // LANES
    rows = 2 * SUBLANES

    @pl.when(j == 0)
    def _route():
        xb = xt_ref[...].astype(BF16)
        xb_scr[...] = xb
        acc_scr[...] = jnp.zeros_like(acc_scr)
        tables = (rank2_scr, nsel_scr, e1_scr, e2_scr)

        def route_all(exact):
            return lax.fori_loop(0, PEER_HEADS, lambda h, ok: jnp.minimum(ok, _peer_route_head(
                h, xb, wq_ref, sk_ref, *tables, half=half, exact=exact)), jnp.int32(1))

        unambiguous = route_all(False)

        @pl.when(unambiguous == 0)
        def _():
            route_all(True)

    xb = xb_scr[...]
    pieces = cpg // 2
    er = 2 * nk

    def row_bf16(tab, h, cc, ls):
        return jnp.broadcast_to(tab[h, j, cc:cc + 1, ls], (rows, LANES)).astype(BF16)

    p_next = jnp.dot(d_ref[0:er, :], xb, preferred_element_type=F32)
    for m in range(pieces):
        p_cur = p_next
        if m + 1 < pieces:
            p_next = jnp.dot(d_ref[(m + 1) * er:(m + 2) * er, :], xb, preferred_element_type=F32)
        for lt in range(nlt):
            ls = slice(lt * LANES, (lt + 1) * LANES)
            nrg = nk // rows
            g = [[jnp.zeros((rows, LANES), BF16) for _ in range(2)] for _ in range(nrg)]
            for h in range(PEER_HEADS):
                nrow = [row_bf16(nsel_scr, h, 2 * m + c2, ls) for c2 in range(2)]
                erow = [row_bf16(e1_scr, h, 2 * m + c2, ls) for c2 in range(2)]
                for rg in range(nrg):
                    r2 = rank2_scr[h, rg * rows:(rg + 1) * rows, ls]
                    e2 = e2_scr[h, rg * rows:(rg + 1) * rows, ls]
                    for c2 in range(2):
                        w = jnp.where(r2 < nrow[c2], e2, jnp.zeros_like(e2))
                        g[rg][c2] = g[rg][c2] + w * erow[c2]
            for rg in range(nrg):
                for c2 in range(2):
                    act = _gelu_tanh(p_cur[c2 * nk + rg * rows:c2 * nk + (rg + 1) * rows, ls])
                    a0 = (2 * m + c2) * nk + rg * rows
                    a_scr[a0:a0 + rows, ls] = g[rg][c2] * act.astype(BF16)
        if m % 2 == 1:
            e0 = (m - 1) * er
            acc_scr[...] += jnp.dot(ut_ref[:, e0:e0 + 2 * er], a_scr[e0:e0 + 2 * er, :], preferred_element_type=F32)

    @pl.when(j == pl.num_programs(1) - 1)
    def _finish():
        y = (alpha * xt_ref[...] + acc_scr[...]).T
        o_ref[...] = _layer_norm(y, g2_ref[...], b2_ref[...])


def _peer(h1t, wq_bf, sk_bf, down_bf, upt_bf, g2, b2, *, alpha, tm, cpg):
    d, n = h1t.shape
    nk, half = sk_bf.shape[2], sk_bf.shape[3]
    n_exp = down_bf.shape[0]
    assert cpg == SUBLANES and nk % (2 * SUBLANES) == 0
    eg = cpg * nk
    ng = n_exp // eg
    body = functools.partial(_peer_body, alpha=alpha, nk=nk, half=half, tm=tm, cpg=cpg)
    head_scr = lambda: pltpu.VMEM((PEER_HEADS, nk, tm), BF16)
    group_scr = lambda: pltpu.VMEM((PEER_HEADS, nk // cpg, cpg, tm), F32)
    return pl.pallas_call(
        body,
        grid=(n // tm, ng),
        in_specs=[
            pl.BlockSpec((d, tm), lambda i, j: (0, i), pipeline_mode=pl.Buffered(1)),
            pl.BlockSpec(wq_bf.shape, lambda i, j: (0, 0), pipeline_mode=pl.Buffered(1)),
            pl.BlockSpec(sk_bf.shape, lambda i, j: (0, 0, 0, 0), pipeline_mode=pl.Buffered(1)),
            pl.BlockSpec((eg, d), lambda i, j: (j, 0)),
            pl.BlockSpec((d, eg), lambda i, j: (0, j)),
            pl.BlockSpec((1, d), lambda i, j: (0, 0), pipeline_mode=pl.Buffered(1)),
            pl.BlockSpec((1, d), lambda i, j: (0, 0), pipeline_mode=pl.Buffered(1)),
        ],
        out_specs=pl.BlockSpec((tm, d), lambda i, j: (i, 0)),
        out_shape=jax.ShapeDtypeStruct((n, d), F32),
        scratch_shapes=[
            pltpu.VMEM((d, tm), BF16),
            head_scr(), group_scr(), group_scr(), head_scr(),
            pltpu.VMEM((eg, tm), BF16),
            pltpu.VMEM((d, tm), F32),
        ],
        compiler_params=pltpu.CompilerParams(
            dimension_semantics=("arbitrary", "arbitrary"), vmem_limit_bytes=VMEM_LIMIT),
        name="peer",
    )(h1t, wq_bf, sk_bf, down_bf, upt_bf, g2, b2)


def _pick_tile(n, candidates):
    for t in candidates:
        if n % t == 0:
            return t
    raise ValueError(f"no tile for {n}")


def _block_diag(w):
    nh, di, dj = w.shape
    eye = jnp.eye(nh, dtype=w.dtype)
    return (eye[:, None, :, None] * w[:, :, None, :]).reshape(nh * di, nh * dj)


def kernel(x, meta_tokens, ln_emb_g, ln_emb_b, w_in, conv_w, conv_b, gate_a_w, gate_a_b, gate_x_w, gate_x_b,
           lru_lambda, attn_norm_g, lru_norm_g, w_out, ln1_g, ln1_b, peer_query_w, peer_sub_keys, peer_down,
           peer_up, ln2_g, ln2_b):
    bsz, seq, d = x.shape
    depth = w_in.shape[0]
    assert depth == 1, "single-layer stack only"
    n_meta = meta_tokens.shape[0]
    aw = attn_norm_g.shape[-1]
    lw = lru_norm_g.shape[-1]
    dh = aw // ATTN_HEADS
    lead = ATTN_QB
    assert 2 * dh == LANES and seq % (ATTN_QB * BLK) == 0 and n_meta <= BLK
    n_pad = lead * BLK - n_meta
    alpha = (2.0 * depth) ** 0.25
    row = lambda v: v.reshape(1, -1)

    head = jnp.concatenate([jnp.zeros((n_pad, d), x.dtype), meta_tokens.astype(x.dtype)], axis=0)

    q, k, v, xr, gr = _inproj(
        head, x, row(ln_emb_g), row(ln_emb_b), w_in[0].astype(BF16),
        n_pad=n_pad, tm=_pick_tile(lead * BLK, (512, 256, 128)), aw=aw, lw=lw, q_scale=dh ** -0.5)

    attn = _attn(q, k, v, dh=dh, qb=ATTN_QB)

    wg = jnp.concatenate([_block_diag(gate_a_w[0]), _block_diag(gate_x_w[0])], axis=1).astype(BF16)
    bg = jnp.concatenate([gate_a_b[0], gate_x_b[0]]).reshape(1, -1)
    lru = _lru(xr, gr, conv_w[0], row(conv_b[0]), wg, bg, row(lru_lambda[0]), n_pad=n_pad, lead=lead)

    h1t = _outproj(attn, lru, x, row(attn_norm_g[0]), row(lru_norm_g[0]), w_out[0].astype(BF16),
                   row(ln_emb_g), row(ln_emb_b), row(ln1_g[0]), row(ln1_b[0]),
                   alpha=alpha, tm=_pick_tile(seq, (512, 256, 128)))

    nk = peer_sub_keys.shape[3]
    wq = peer_query_w[0].reshape(d, -1).T.astype(BF16)
    out = _peer(h1t, wq, peer_sub_keys[0].astype(BF16), peer_down[0].astype(BF16),
                peer_up[0].T.astype(BF16), row(ln2_g[0]), row(ln2_b[0]),
                alpha=alpha, tm=_pick_tile(bsz * seq, (1024, 512, 256, 128)), cpg=SUBLANES)
    return out.reshape(bsz, seq, d)
```
